```python
import math
import jax
import jax.numpy as jnp
from jax import lax
import numpy as np

D_MODEL = 1024
BATCH = 16
SEQ = 2048
DEPTH = 4

GRID_W = 64
CTX_LEN = 256
N_MIXERS = 4
RMS_EPS = 1e-6
D_FF = 4 * D_MODEL

GA_HEAD_DIM = 128
GA_N_HEADS = D_MODEL // GA_HEAD_DIM
GA_N_KV = 2
GA_GROUP = GA_N_HEADS // GA_N_KV
ROPE_THETA = 10000.0
Q_BLOCK = 128

HY_ORDER = 2
HY_SHORT = 3
HY_FILTER_HIDDEN = 64
HY_BANDS = 16
HY_EMB = 1 + 2 * HY_BANDS
HY_DECAY_SHORT = 0.3
HY_DECAY_LONG = 1.5
HY_TARGET = 1e-2

SSM_D_INNER = 2 * D_MODEL
SSM_HEAD_DIM = 64
SSM_N_HEADS = SSM_D_INNER // SSM_HEAD_DIM
SSM_GROUPS = 4
SSM_D_STATE = 128
SSM_CONV = 3
SSM_CHUNK = 128
SSM_GN = SSM_GROUPS * SSM_D_STATE
SSM_IN_DIM = 2 * SSM_D_INNER + 2 * SSM_GN + 2 * SSM_N_HEADS

NA_HEAD_DIM = 64
NA_N_HEADS = D_MODEL // NA_HEAD_DIM
NA_KH = 8
NA_KW = 16

kernel_name = 'hybrid_dit_interleaved_prefix_ctx'


def rmsnorm(x, g):
    xf = x.astype(jnp.float32)
    y = xf * lax.rsqrt(jnp.mean(xf * xf, axis=-1, keepdims=True) + RMS_EPS)
    return (y * g.astype(jnp.float32)).astype(x.dtype)


def modulate(h, shift, scale):
    return h * (1 + scale) + shift


def sq_relu_mlp(h, w1, w2):
    return jnp.square(jax.nn.relu(h @ w1)) @ w2


def dwconv_centred(u, w, b):
    k, ch = w.shape
    y = lax.conv_general_dilated(u, w[:, None, :].astype(u.dtype), window_strides=(1,),
                                 padding=[(k // 2, k // 2)],
                                 dimension_numbers=('NWC', 'WIO', 'NWC'), feature_group_count=ch)
    return y + b


def axial_rope_tables(n_tokens, head_dim):
    t = jnp.arange(n_tokens)
    row = (t // GRID_W).astype(jnp.float32)
    col = (t % GRID_W).astype(jnp.float32)
    nf = head_dim // 4
    inv = ROPE_THETA ** (-jnp.arange(nf, dtype=jnp.float32) / nf)
    ar = row[:, None] * inv[None, :]
    ac = col[:, None] * inv[None, :]
    cos = jnp.concatenate([jnp.cos(ar), jnp.cos(ar), jnp.cos(ac), jnp.cos(ac)], axis=-1)
    sin = jnp.concatenate([jnp.sin(ar), jnp.sin(ar), jnp.sin(ac), jnp.sin(ac)], axis=-1)
    return cos, sin


def apply_rope(x, cos, sin):
    shape = (x.shape[1],) + (1,) * (x.ndim - 3) + (x.shape[-1],)
    cos = cos.reshape(shape)
    sin = sin.reshape(shape)
    a, b, c, d = jnp.split(x, 4, axis=-1)
    rot = jnp.concatenate([-b, a, -d, c], axis=-1)
    return (x * cos + rot * sin).astype(x.dtype)


def gqa_attend(q, k, v):
    s = jnp.einsum('bqkgd,bskd->bkgqs', q, k).astype(jnp.float32) * (q.shape[-1] ** -0.5)
    p = jax.nn.softmax(s, axis=-1).astype(v.dtype)
    return jnp.einsum('bkgqs,bskd->bqkgd', p, v)


def mha(q, k, v):
    s = jnp.einsum('bqhd,bshd->bhqs', q, k).astype(jnp.float32) * (q.shape[-1] ** -0.5)
    p = jax.nn.softmax(s, axis=-1).astype(v.dtype)
    return jnp.einsum('bhqs,bshd->bqhd', p, v)


def mixer_gqa(h_lat, h_ctx, w_in, q_gain, k_gain, w_out, with_ctx_out):
    bsz, n_lat, _ = h_lat.shape
    hd = GA_HEAD_DIM

    def qkv(h):
        n = h.shape[1]
        q, k, v = jnp.split(h @ w_in, [GA_N_HEADS * hd, (GA_N_HEADS + GA_N_KV) * hd], axis=-1)
        q = rmsnorm(q.reshape(bsz, n, GA_N_KV, GA_GROUP, hd), q_gain)
        k = rmsnorm(k.reshape(bsz, n, GA_N_KV, hd), k_gain)
        return q, k, v.reshape(bsz, n, GA_N_KV, hd)

    q_l, k_l, v_l = qkv(h_lat)
    q_c, k_c, v_c = qkv(h_ctx)
    cos, sin = axial_rope_tables(n_lat, hd)
    q_l = apply_rope(q_l, cos, sin)
    k_l = apply_rope(k_l, cos, sin)
    k_all = jnp.concatenate([k_l, k_c], axis=1)
    v_all = jnp.concatenate([v_l, v_c], axis=1)
    nb = n_lat // Q_BLOCK
    qb = q_l.reshape(bsz, nb, Q_BLOCK, GA_N_KV, GA_GROUP, hd).swapaxes(0, 1)
    o = lax.map(lambda qblk: gqa_attend(qblk, k_all, v_all), qb)
    y_lat = o.swapaxes(0, 1).reshape(bsz, n_lat, GA_N_HEADS * hd) @ w_out
    y_ctx = None
    if with_ctx_out:
        y_ctx = gqa_attend(q_c, k_c, v_c).reshape(bsz, -1, GA_N_HEADS * hd) @ w_out
    return y_lat, y_ctx


def hyena_filters(n, w1, b1, w2, b2, w3, freq):
    t = jnp.linspace(0.0, 1.0, n, dtype=jnp.float32)[:, None]
    w = 2.0 * math.pi * jnp.arange(n, dtype=jnp.float32)[:, None] / n
    bands = jnp.linspace(1e-4, HY_BANDS - 1, HY_BANDS, dtype=jnp.float32)[None, :]
    feats = jnp.concatenate([t, jnp.cos(bands * w), -jnp.sin(bands * w)], axis=-1)
    h = jnp.sin(freq[0] * (feats @ w1 + b1))
    h = jnp.sin(freq[1] * (h @ w2 + b2))
    h = (h @ w3).astype(jnp.float32).reshape(n, 2 * HY_ORDER, D_MODEL)
    deltas = jnp.abs(jnp.linspace(math.log(HY_TARGET) / HY_DECAY_SHORT,
                                  math.log(HY_TARGET) / HY_DECAY_LONG, D_MODEL, dtype=jnp.float32))
    return h * jnp.exp(-t * deltas[None, :])[:, None, :]


def bidir_long_conv(z, h_fwd, h_bwd, skip):
    n = z.shape[1]
    filt_full = jnp.concatenate([h_fwd, jnp.zeros_like(h_fwd[:1]), h_bwd[:0:-1]], axis=0)
    zf = jnp.fft.rfft(z.astype(jnp.float32), n=2 * n, axis=1)
    ff = jnp.fft.rfft(filt_full, n=2 * n, axis=0)
    y = jnp.fft.irfft(zf * ff[None], n=2 * n, axis=1)[:, :n]
    return (y + z.astype(jnp.float32) * skip.astype(jnp.float32)).astype(z.dtype)


def mixer_hyena(h_lat, h_ctx, w_in, conv_w, conv_b, f_w1, f_b1, f_w2, f_b2, f_w3, f_freq, skip, w_out,
                with_ctx_out):
    def run(h):
        u = dwconv_centred(h @ w_in, conv_w, conv_b)
        v, x1, x2 = jnp.split(u, HY_ORDER + 1, axis=-1)
        filt = hyena_filters(h.shape[1], f_w1, f_b1, f_w2, f_b2, f_w3, f_freq)
        z = x1 * bidir_long_conv(v, filt[:, 0], filt[:, 1], skip[0])
        z = x2 * bidir_long_conv(z, filt[:, 2], filt[:, 3], skip[1])
        return z @ w_out

    y_ctx = run(h_ctx) if with_ctx_out else None
    return run(h_lat), y_ctx


def segsum_exp(a_cs):
    t = a_cs.shape[-1]
    diff = a_cs[..., :, None] - a_cs[..., None, :]
    return jnp.where(jnp.tril(jnp.ones((t, t), bool)), jnp.exp(jnp.minimum(diff, 0.0)), 0.0)


def ssd_scan(xs, dt, a, bm, cm, init_state):
    bsz, n, nh, p = xs.shape
    g, ns = bm.shape[2], bm.shape[3]
    e = nh // g
    nc = n // SSM_CHUNK
    x = (xs.astype(jnp.float32) * dt[..., None]).reshape(bsz, nc, SSM_CHUNK, g, e, p)
    da = (dt * a).reshape(bsz, nc, SSM_CHUNK, g, e).transpose(0, 3, 4, 1, 2)
    bc = bm.astype(jnp.float32).reshape(bsz, nc, SSM_CHUNK, g, ns)
    cc = cm.astype(jnp.float32).reshape(bsz, nc, SSM_CHUNK, g, ns)
    a_cs = jnp.cumsum(da, axis=-1)
    lmat = segsum_exp(a_cs)
    cb = jnp.einsum('bclgn,bcsgn->bgcls', cc, bc)
    y_diag = jnp.einsum('bgcls,bgecls,bcsgep->bclgep', cb, lmat, x)
    decay_states = jnp.exp(a_cs[..., -1:] - a_cs)
    states = jnp.einsum('bcsgn,bgecs,bcsgep->bcgepn', bc, decay_states, x)
    init = init_state.astype(jnp.float32).reshape(bsz, 1, g, e, p, ns)
    states = jnp.concatenate([init, states], axis=1)
    totals = jnp.pad(a_cs[..., -1], ((0, 0), (0, 0), (0, 0), (1, 0)))
    chunk_decay = segsum_exp(jnp.cumsum(totals, axis=-1))
    new_states = jnp.einsum('bgezc,bcgepn->bzgepn', chunk_decay, states)
    states, final = new_states[:, :-1], new_states[:, -1]
    y_off = jnp.einsum('bclgn,bcgepn,bgecl->bclgep', cc, states, jnp.exp(a_cs))
    y = (y_diag + y_off).reshape(bsz, n, nh, p)
    return y, final.reshape(bsz, nh, p, ns)


def mixer_ssd(h_lat, h_ctx, w_in, conv_w, conv_b, dt_bias, a_log, d_skip, norm_g, w_out, with_ctx_out):
    bsz = h_lat.shape[0]

    def prep(h):
        n = h.shape[1]
        z, xbc, dt = jnp.split(h @ w_in, [SSM_D_INNER, 2 * SSM_D_INNER + 2 * SSM_GN], axis=-1)
        xbc = jax.nn.silu(dwconv_centred(xbc, conv_w, conv_b))
        xs, bm, cm = jnp.split(xbc, [SSM_D_INNER, SSM_D_INNER + SSM_GN], axis=-1)
        xs = xs.reshape(bsz, n, SSM_N_HEADS, SSM_HEAD_DIM)
        bm = bm.reshape(bsz, n, SSM_GROUPS, SSM_D_STATE)
        cm = cm.reshape(bsz, n, SSM_GROUPS, SSM_D_STATE)
        dt = jax.nn.softplus(dt.astype(jnp.float32).reshape(bsz, n, 2, SSM_N_HEADS)
                             + dt_bias.astype(jnp.float32))
        return z, xs, bm, cm, dt

    a = -jnp.exp(a_log.astype(jnp.float32))

    def bidir(xs, bm, cm, dt, s_f, s_b):
        y_f, fin_f = ssd_scan(xs, dt[:, :, 0], a[0], bm, cm, s_f)
        y_b, fin_b = ssd_scan(xs[:, ::-1], dt[:, ::-1, 1], a[1], bm[:, ::-1], cm[:, ::-1], s_b)
        return y_f + y_b[:, ::-1], fin_f, fin_b

    def out(z, xs, y):
        n = xs.shape[1]
        y = (y + xs.astype(jnp.float32) * d_skip.astype(jnp.float32)[:, None]).reshape(bsz, n, SSM_D_INNER)
        y = rmsnorm(y * jax.nn.silu(z.astype(jnp.float32)), norm_g).astype(z.dtype)
        return y @ w_out

    zc, xc, bcm, ccm, dtc = prep(h_ctx)
    zero = jnp.zeros((bsz, SSM_N_HEADS, SSM_HEAD_DIM, SSM_D_STATE), jnp.float32)
    y_c, s_f, s_b = bidir(xc, bcm, ccm, dtc, zero, zero)
    zl, xl, blm, clm, dtl = prep(h_lat)
    y_l, _, _ = bidir(xl, blm, clm, dtl, s_f, s_b)
    y_ctx = out(zc, xc, y_c) if with_ctx_out else None
    return out(zl, xl, y_l), y_ctx


def mixer_na(h_lat, h_ctx, w_in, rpb, w_out, with_ctx_out):
    bsz, n_lat, _ = h_lat.shape
    rows = n_lat // GRID_W
    kh = min(NA_KH, rows)
    scale = NA_HEAD_DIM ** -0.5

    def qkv(h):
        q, k, v = jnp.split(h @ w_in, 3, axis=-1)
        shp = (bsz, h.shape[1], NA_N_HEADS, NA_HEAD_DIM)
        return q.reshape(shp), k.reshape(shp), v.reshape(shp)

    q_l, k_l, v_l = qkv(h_lat)
    q_c, k_c, v_c = qkv(h_ctx)
    grid = (bsz, rows, GRID_W, NA_N_HEADS, NA_HEAD_DIM)
    q_g = q_l.reshape(grid).swapaxes(0, 1)
    k_g = k_l.reshape(grid)
    v_g = v_l.reshape(grid)
    col = jnp.arange(GRID_W)
    c0 = jnp.clip(col - NA_KW // 2, 0, GRID_W - NA_KW)
    col_ok = (col[None, :] >= c0[:, None]) & (col[None, :] < c0[:, None] + NA_KW)
    dx_idx = jnp.clip(col[None, :] - col[:, None] + NA_KW - 1, 0, 2 * NA_KW - 2)
    rpb_cols = rpb.astype(jnp.float32)[:, :, dx_idx]

    def row_block(args):
        r, q_r = args
        r0 = jnp.clip(r - kh // 2, 0, rows - kh)
        k_band = lax.dynamic_slice_in_dim(k_g, r0, kh, axis=1)
        v_band = lax.dynamic_slice_in_dim(v_g, r0, kh, axis=1)
        dy_idx = r0 + jnp.arange(kh) - r + NA_KH - 1
        bias = rpb_cols[:, dy_idx].transpose(0, 2, 1, 3)
        s_lat = jnp.einsum('bqhd,byxhd->bhqyx', q_r, k_band).astype(jnp.float32) * scale + bias[None]
        s_lat = jnp.where(col_ok[:, None, :], s_lat, -jnp.inf)
        s_ctx = jnp.einsum('bqhd,bshd->bhqs', q_r, k_c).astype(jnp.float32) * scale
        s = jnp.concatenate([s_lat.reshape(bsz, NA_N_HEADS, GRID_W, kh * GRID_W), s_ctx], axis=-1)
        p = jax.nn.softmax(s, axis=-1).astype(v_band.dtype)
        p_lat = p[..., :kh * GRID_W].reshape(bsz, NA_N_HEADS, GRID_W, kh, GRID_W)
        p_ctx = p[..., kh * GRID_W:]
        return (jnp.einsum('bhqyx,byxhd->bqhd', p_lat, v_band)
                + jnp.einsum('bhqs,bshd->bqhd', p_ctx, v_c))

    o = lax.map(row_block, (jnp.arange(rows), q_g))
    y_lat = o.swapaxes(0, 1).reshape(bsz, n_lat, D_MODEL) @ w_out
    y_ctx = None
    if with_ctx_out:
        y_ctx = mha(q_c, k_c, v_c).reshape(bsz, -1, D_MODEL) @ w_out
    return y_lat, y_ctx


def sandwich_update(x, y_mix, mod, g, w1, w2):
    x = x + mod[2] * rmsnorm(y_mix, g[1])
    h = modulate(rmsnorm(x, g[2]), mod[3], mod[4])
    return x + mod[5] * rmsnorm(sq_relu_mlp(h, w1, w2), g[3])


def setup_inputs(seed: int = 0) -> dict:
    key = jax.random.key(seed)
    keys = iter(jax.random.split(key, 40))

    def nrm(shape, scale):
        return jax.random.normal(next(keys), shape, jnp.float32) * scale

    def unif(shape, lo, hi):
        return jax.random.uniform(next(keys), shape, jnp.float32, lo, hi)

    n_a, n_b, n_c, n_d = (len(range(m, DEPTH, N_MIXERS)) for m in range(N_MIXERS))
    d = D_MODEL
    dt0 = jnp.exp(unif((n_c, 2, SSM_N_HEADS), math.log(1e-3), math.log(1e-1)))
    return {
        'x': nrm((BATCH, SEQ, d), 1.0),
        'c': nrm((BATCH, d), 1.0),
        'ctx': nrm((BATCH, CTX_LEN, d), 1.0),
        'c_ctx': nrm((d,), 1.0),
        'ada_w': nrm((DEPTH, d, 6 * d), 0.5 * d ** -0.5),
        'ada_b': nrm((DEPTH, 6 * d), 0.02),
        'norm_g': 1.0 + nrm((DEPTH, 4, d), 0.02),
        'mlp_w1': nrm((DEPTH, d, D_FF), d ** -0.5),
        'mlp_w2': nrm((DEPTH, D_FF, d), D_FF ** -0.5),
        'ga_w_in': nrm((n_a, d, (GA_N_HEADS + 2 * GA_N_KV) * GA_HEAD_DIM), d ** -0.5),
        'ga_q_gain': 1.0 + nrm((n_a, GA_HEAD_DIM), 0.02),
        'ga_k_gain': 1.0 + nrm((n_a, GA_HEAD_DIM), 0.02),
        'ga_w_out': nrm((n_a, GA_N_HEADS * GA_HEAD_DIM, d), (GA_N_HEADS * GA_HEAD_DIM) ** -0.5),
        'hy_w_in': nrm((n_b, d, (HY_ORDER + 1) * d), d ** -0.5),
        'hy_conv_w': nrm((n_b, HY_SHORT, (HY_ORDER + 1) * d), HY_SHORT ** -0.5),
        'hy_conv_b': nrm((n_b, (HY_ORDER + 1) * d), 0.02),
        'hy_f_w1': nrm((n_b, HY_EMB, HY_FILTER_HIDDEN), HY_EMB ** -0.5),
        'hy_f_b1': nrm((n_b, HY_FILTER_HIDDEN), 0.02),
        'hy_f_w2': nrm((n_b, HY_FILTER_HIDDEN, HY_FILTER_HIDDEN), HY_FILTER_HIDDEN ** -0.5),
        'hy_f_b2': nrm((n_b, HY_FILTER_HIDDEN), 0.02),
        'hy_f_w3': nrm((n_b, HY_FILTER_HIDDEN, 2 * HY_ORDER * d), HY_FILTER_HIDDEN ** -0.5),
        'hy_f_freq': 1.0 + nrm((n_b, 2, HY_FILTER_HIDDEN), 0.02),
        'hy_skip': nrm((n_b, HY_ORDER, d), 0.5),
        'hy_w_out': nrm((n_b, d, d), d ** -0.5),
        'ssm_w_in': nrm((n_c, d, SSM_IN_DIM), d ** -0.5),
        'ssm_conv_w': nrm((n_c, SSM_CONV, SSM_D_INNER + 2 * SSM_GN), SSM_CONV ** -0.5),
        'ssm_conv_b': nrm((n_c, SSM_D_INNER + 2 * SSM_GN), 0.02),
        'ssm_dt_bias': dt0 + jnp.log(-jnp.expm1(-dt0)),
        'ssm_a_log': jnp.log(unif((n_c, 2, SSM_N_HEADS), 1.0, 16.0)),
        'ssm_d': 1.0 + nrm((n_c, SSM_N_HEADS), 0.02),
        'ssm_norm_g': 1.0 + nrm((n_c, SSM_D_INNER), 0.02),
        'ssm_w_out': nrm((n_c, SSM_D_INNER, d), SSM_D_INNER ** -0.5),
        'na_w_in': nrm((n_d, d, 3 * d), d ** -0.5),
        'na_rpb': nrm((n_d, NA_N_HEADS, 2 * NA_KH - 1, 2 * NA_KW - 1), 0.02),
        'na_w_out': nrm((n_d, d, d), d ** -0.5),
    }


def reference(x, c, ctx, c_ctx, ada_w, ada_b, norm_g, mlp_w1, mlp_w2,
              ga_w_in, ga_q_gain, ga_k_gain, ga_w_out,
              hy_w_in, hy_conv_w, hy_conv_b, hy_f_w1, hy_f_b1, hy_f_w2, hy_f_b2, hy_f_w3, hy_f_freq,
              hy_skip, hy_w_out,
              ssm_w_in, ssm_conv_w, ssm_conv_b, ssm_dt_bias, ssm_a_log, ssm_d, ssm_norm_g, ssm_w_out,
              na_w_in, na_rpb, na_w_out):
    sc = jax.nn.silu(c)
    scc = jax.nn.silu(c_ctx)
    x_lat, x_ctx = x, ctx
    for i in range(DEPTH):
        kind, j = i % N_MIXERS, i // N_MIXERS
        with_ctx = i < DEPTH - 1
        mod_l = jnp.split((sc @ ada_w[i] + ada_b[i])[:, None, :], 6, axis=-1)
        mod_c = jnp.split((scc @ ada_w[i] + ada_b[i])[None, None, :], 6, axis=-1)
        h_l = modulate(rmsnorm(x_lat, norm_g[i, 0]), mod_l[0], mod_l[1])
        h_c = modulate(rmsnorm(x_ctx, norm_g[i, 0]), mod_c[0], mod_c[1])
        if kind == 0:
            y_l, y_c = mixer_gqa(h_l, h_c, ga_w_in[j], ga_q_gain[j], ga_k_gain[j], ga_w_out[j], with_ctx)
        elif kind == 1:
            y_l, y_c = mixer_hyena(h_l, h_c, hy_w_in[j], hy_conv_w[j], hy_conv_b[j], hy_f_w1[j], hy_f_b1[j],
                                   hy_f_w2[j], hy_f_b2[j], hy_f_w3[j], hy_f_freq[j], hy_skip[j],
                                   hy_w_out[j], with_ctx)
        elif kind == 2:
            y_l, y_c = mixer_ssd(h_l, h_c, ssm_w_in[j], ssm_conv_w[j], ssm_conv_b[j], ssm_dt_bias[j],
                                 ssm_a_log[j], ssm_d[j], ssm_norm_g[j], ssm_w_out[j], with_ctx)
        else:
            y_l, y_c = mixer_na(h_l, h_c, na_w_in[j], na_rpb[j], na_w_out[j], with_ctx)
        x_lat = sandwich_update(x_lat, y_l, mod_l, norm_g[i], mlp_w1[i], mlp_w2[i])
        if with_ctx:
            x_ctx = sandwich_update(x_ctx, y_c, mod_c, norm_g[i], mlp_w1[i], mlp_w2[i])
    return x_lat
```

```python
import functools
import math

import jax
import jax.numpy as jnp
from jax import lax
from jax.experimental import pallas as pl
from jax.experimental.pallas import tpu as pltpu

GRID_W = 64
N_MIXERS = 4
RMS_EPS = 1e-6

GA_HEAD_DIM = 128
GA_N_KV = 2
ROPE_THETA = 10000.0
Q_BLOCK = 128

HY_ORDER = 2
HY_BANDS = 16
HY_DECAY_SHORT = 0.3
HY_DECAY_LONG = 1.5
HY_TARGET = 1e-2

SSM_HEAD_DIM = 64
SSM_GROUPS = 4
SSM_D_STATE = 128
SSM_CHUNK = 128

NA_HEAD_DIM = 64
NA_KH = 8
NA_KW = 16

VMEM_LIMIT_BYTES = 56 * 1024 * 1024
BF16 = jnp.bfloat16
F32 = jnp.float32


def _params(n_grid_dims):
    return pltpu.CompilerParams(dimension_semantics=("arbitrary",) * n_grid_dims,
                                vmem_limit_bytes=VMEM_LIMIT_BYTES)


def _row_tile(rows_per_mod, cap):
    t = min(rows_per_mod, cap)
    while rows_per_mod % t:
        t //= 2
    return t


def _col_chunks(n, width):
    return [(c, min(width, n - c)) for c in range(0, n, width)]


def _rms(x, g):
    return x * lax.rsqrt(jnp.mean(x * x, axis=-1, keepdims=True) + RMS_EPS) * g


def _dense_kernel(a_ref, w_ref, b_ref, o_ref):
    o_ref[...] = jnp.dot(a_ref[...], w_ref[...], preferred_element_type=F32,
                         precision=lax.Precision.HIGHEST) + b_ref[...]


def dense_f32(a, w, b):
    m, k = a.shape
    n = w.shape[1]
    tn = 1024 if n % 1024 == 0 else n
    return pl.pallas_call(
        _dense_kernel,
        grid=(n // tn,),
        in_specs=[pl.BlockSpec((m, k), lambda j: (0, 0)),
                  pl.BlockSpec((k, tn), lambda j: (0, j)),
                  pl.BlockSpec((1, tn), lambda j: (0, j))],
        out_specs=pl.BlockSpec((m, tn), lambda j: (0, j)),
        out_shape=jax.ShapeDtypeStruct((m, n), F32),
        compiler_params=_params(1),
    )(a, w, b)


def _norm_proj_kernel(x_ref, mod_ref, g_ref, w_ref, o_ref, *, tn):
    x = x_ref[...]
    h = _rms(x, g_ref[0:1, :]) * (1.0 + mod_ref[0, 1:2, :]) + mod_ref[0, 0:1, :]
    h = h.astype(BF16)
    for c, s in _col_chunks(w_ref.shape[1], tn):
        o_ref[:, c:c + s] = jnp.dot(h, w_ref[:, c:c + s], preferred_element_type=F32).astype(o_ref.dtype)


def norm_proj(x, mod, g, w, rows_per_mod, out_dtype=F32):
    m, d = x.shape
    n = w.shape[1]
    tm = _row_tile(rows_per_mod, 512)
    bpm = rows_per_mod // tm
    return pl.pallas_call(
        functools.partial(_norm_proj_kernel, tn=512),
        grid=(m // tm,),
        in_specs=[pl.BlockSpec((tm, d), lambda i: (i, 0)),
                  pl.BlockSpec((1, 6, d), lambda i: (i // bpm, 0, 0)),
                  pl.BlockSpec((4, d), lambda i: (0, 0)),
                  pl.BlockSpec((d, n), lambda i: (0, 0))],
        out_specs=pl.BlockSpec((tm, n), lambda i: (i, 0)),
        out_shape=jax.ShapeDtypeStruct((m, n), out_dtype),
        compiler_params=_params(1),
    )(x, mod, g, w)


def _out_proj_kernel(a_ref, w_ref, x_ref, mod_ref, g_ref, xo_ref, h_ref):
    y = jnp.dot(a_ref[...].astype(BF16), w_ref[...], preferred_element_type=F32)
    x = x_ref[...] + mod_ref[0, 2:3, :] * _rms(y, g_ref[1:2, :])
    xo_ref[...] = x
    h = _rms(x, g_ref[2:3, :]) * (1.0 + mod_ref[0, 4:5, :]) + mod_ref[0, 3:4, :]
    h_ref[...] = h.astype(BF16)


def out_proj_residual(a, w, x, mod, g, rows_per_mod):
    m, k = a.shape
    d = w.shape[1]
    tm = _row_tile(rows_per_mod, 512)
    bpm = rows_per_mod // tm
    return pl.pallas_call(
        _out_proj_kernel,
        grid=(m // tm,),
        in_specs=[pl.BlockSpec((tm, k), lambda i: (i, 0)),
                  pl.BlockSpec((k, d), lambda i: (0, 0)),
                  pl.BlockSpec((tm, d), lambda i: (i, 0)),
                  pl.BlockSpec((1, 6, d), lambda i: (i // bpm, 0, 0)),
                  pl.BlockSpec((4, d), lambda i: (0, 0))],
        out_specs=[pl.BlockSpec((tm, d), lambda i: (i, 0)),
                   pl.BlockSpec((tm, d), lambda i: (i, 0))],
        out_shape=[jax.ShapeDtypeStruct((m, d), F32), jax.ShapeDtypeStruct((m, d), BF16)],
        compiler_params=_params(1),
    )(a, w, x, mod, g)


def _mlp_kernel(h_ref, w1_ref, w2_ref, x_ref, mod_ref, g_ref, o_ref, *, fc):
    h = h_ref[...]
    acc = jnp.zeros(o_ref.shape, F32)
    for c, s in _col_chunks(w1_ref.shape[1], fc):
        u = jnp.dot(h, w1_ref[:, c:c + s], preferred_element_type=F32)
        u = jnp.square(jnp.maximum(u, 0.0)).astype(BF16)
        acc = acc + jnp.dot(u, w2_ref[c:c + s, :], preferred_element_type=F32)
    o_ref[...] = x_ref[...] + mod_ref[0, 5:6, :] * _rms(acc, g_ref[3:4, :])


def mlp_residual(h, w1, w2, x, mod, g, rows_per_mod):
    m, d = x.shape
    f = w1.shape[1]
    tm = _row_tile(rows_per_mod, 512)
    bpm = rows_per_mod // tm
    return pl.pallas_call(
        functools.partial(_mlp_kernel, fc=1024),
        grid=(m // tm,),
        in_specs=[pl.BlockSpec((tm, d), lambda i: (i, 0)),
                  pl.BlockSpec((d, f), lambda i: (0, 0)),
                  pl.BlockSpec((f, d), lambda i: (0, 0)),
                  pl.BlockSpec((tm, d), lambda i: (i, 0)),
                  pl.BlockSpec((1, 6, d), lambda i: (i // bpm, 0, 0)),
                  pl.BlockSpec((4, d), lambda i: (0, 0))],
        out_specs=pl.BlockSpec((tm, d), lambda i: (i, 0)),
        out_shape=jax.ShapeDtypeStruct((m, d), F32),
        compiler_params=_params(1),
    )(h, w1, w2, x, mod, g)


def rmsnorm(x, g):
    xf = x.astype(jnp.float32)
    y = xf * lax.rsqrt(jnp.mean(xf * xf, axis=-1, keepdims=True) + RMS_EPS)
    return (y * g.astype(jnp.float32)).astype(x.dtype)


def dwconv_centred(u, w, b):
    k, ch = w.shape
    y = lax.conv_general_dilated(u, w[:, None, :].astype(u.dtype), window_strides=(1,),
                                 padding=[(k // 2, k // 2)],
                                 dimension_numbers=('NWC', 'WIO', 'NWC'), feature_group_count=ch)
    return y + b


def axial_rope_tables(n_tokens, head_dim):
    t = jnp.arange(n_tokens)
    row = (t // GRID_W).astype(jnp.float32)
    col = (t % GRID_W).astype(jnp.float32)
    nf = head_dim // 4
    inv = ROPE_THETA ** (-jnp.arange(nf, dtype=jnp.float32) / nf)
    ar = row[:, None] * inv[None, :]
    ac = col[:, None] * inv[None, :]
    cos = jnp.concatenate([jnp.cos(ar), jnp.cos(ar), jnp.cos(ac), jnp.cos(ac)], axis=-1)
    sin = jnp.concatenate([jnp.sin(ar), jnp.sin(ar), jnp.sin(ac), jnp.sin(ac)], axis=-1)
    return cos, sin


def apply_rope(x, cos, sin):
    shape = (x.shape[1],) + (1,) * (x.ndim - 3) + (x.shape[-1],)
    cos = cos.reshape(shape)
    sin = sin.reshape(shape)
    a, b, c, d = jnp.split(x, 4, axis=-1)
    rot = jnp.concatenate([-b, a, -d, c], axis=-1)
    return (x * cos + rot * sin).astype(x.dtype)


def gqa_attend(q, k, v):
    s = jnp.einsum('bqkgd,bskd->bkgqs', q, k).astype(jnp.float32) * (q.shape[-1] ** -0.5)
    p = jax.nn.softmax(s, axis=-1).astype(v.dtype)
    return jnp.einsum('bkgqs,bskd->bqkgd', p, v)


def mha(q, k, v):
    s = jnp.einsum('bqhd,bshd->bhqs', q, k).astype(jnp.float32) * (q.shape[-1] ** -0.5)
    p = jax.nn.softmax(s, axis=-1).astype(v.dtype)
    return jnp.einsum('bhqs,bshd->bqhd', p, v)


def mixer_gqa(u_lat, u_ctx, q_gain, k_gain, with_ctx_out):
    bsz, n_lat, _ = u_lat.shape
    hd = GA_HEAD_DIM
    n_heads = (u_lat.shape[-1] // hd) - 2 * GA_N_KV
    group = n_heads // GA_N_KV

    def qkv(u):
        n = u.shape[1]
        q, k, v = jnp.split(u, [n_heads * hd, (n_heads + GA_N_KV) * hd], axis=-1)
        q = rmsnorm(q.reshape(bsz, n, GA_N_KV, group, hd), q_gain)
        k = rmsnorm(k.reshape(bsz, n, GA_N_KV, hd), k_gain)
        return q, k, v.reshape(bsz, n, GA_N_KV, hd)

    q_l, k_l, v_l = qkv(u_lat)
    q_c, k_c, v_c = qkv(u_ctx)
    cos, sin = axial_rope_tables(n_lat, hd)
    q_l = apply_rope(q_l, cos, sin)
    k_l = apply_rope(k_l, cos, sin)
    k_all = jnp.concatenate([k_l, k_c], axis=1)
    v_all = jnp.concatenate([v_l, v_c], axis=1)
    nb = n_lat // Q_BLOCK
    qb = q_l.reshape(bsz, nb, Q_BLOCK, GA_N_KV, group, hd).swapaxes(0, 1)
    o = lax.map(lambda qblk: gqa_attend(qblk, k_all, v_all), qb)
    y_lat = o.swapaxes(0, 1).reshape(bsz, n_lat, n_heads * hd)
    y_ctx = None
    if with_ctx_out:
        y_ctx = gqa_attend(q_c, k_c, v_c).reshape(bsz, -1, n_heads * hd)
    return y_lat, y_ctx


def hyena_filters(n, d, w1, b1, w2, b2, w3, freq):
    t = jnp.linspace(0.0, 1.0, n, dtype=jnp.float32)[:, None]
    w = 2.0 * math.pi * jnp.arange(n, dtype=jnp.float32)[:, None] / n
    bands = jnp.linspace(1e-4, HY_BANDS - 1, HY_BANDS, dtype=jnp.float32)[None, :]
    feats = jnp.concatenate([t, jnp.cos(bands * w), -jnp.sin(bands * w)], axis=-1)
    h = jnp.sin(freq[0] * (feats @ w1 + b1))
    h = jnp.sin(freq[1] * (h @ w2 + b2))
    h = (h @ w3).astype(jnp.float32).reshape(n, 2 * HY_ORDER, d)
    deltas = jnp.abs(jnp.linspace(math.log(HY_TARGET) / HY_DECAY_SHORT,
                                  math.log(HY_TARGET) / HY_DECAY_LONG, d, dtype=jnp.float32))
    return h * jnp.exp(-t * deltas[None, :])[:, None, :]


def bidir_long_conv(z, h_fwd, h_bwd, skip):
    n = z.shape[1]
    filt_full = jnp.concatenate([h_fwd, jnp.zeros_like(h_fwd[:1]), h_bwd[:0:-1]], axis=0)
    zf = jnp.fft.rfft(z.astype(jnp.float32), n=2 * n, axis=1)
    ff = jnp.fft.rfft(filt_full, n=2 * n, axis=0)
    y = jnp.fft.irfft(zf * ff[None], n=2 * n, axis=1)[:, :n]
    return (y + z.astype(jnp.float32) * skip.astype(jnp.float32)).astype(z.dtype)


def mixer_hyena(u_lat, u_ctx, conv_w, conv_b, f_w1, f_b1, f_w2, f_b2, f_w3, f_freq, skip, with_ctx_out):
    d = u_lat.shape[-1] // (HY_ORDER + 1)

    def run(u):
        u = dwconv_centred(u, conv_w, conv_b)
        v, x1, x2 = jnp.split(u, HY_ORDER + 1, axis=-1)
        filt = hyena_filters(u.shape[1], d, f_w1, f_b1, f_w2, f_b2, f_w3, f_freq)
        z = x1 * bidir_long_conv(v, filt[:, 0], filt[:, 1], skip[0])
        z = x2 * bidir_long_conv(z, filt[:, 2], filt[:, 3], skip[1])
        return z

    y_ctx = run(u_ctx) if with_ctx_out else None
    return run(u_lat), y_ctx


def segsum_exp(a_cs):
    t = a_cs.shape[-1]
    diff = a_cs[..., :, None] - a_cs[..., None, :]
    return jnp.where(jnp.tril(jnp.ones((t, t), bool)), jnp.exp(jnp.minimum(diff, 0.0)), 0.0)


def ssd_scan(xs, dt, a, bm, cm, init_state):
    bsz, n, nh, p = xs.shape
    g, ns = bm.shape[2], bm.shape[3]
    e = nh // g
    nc = n // SSM_CHUNK
    x = (xs.astype(jnp.float32) * dt[..., None]).reshape(bsz, nc, SSM_CHUNK, g, e, p)
    da = (dt * a).reshape(bsz, nc, SSM_CHUNK, g, e).transpose(0, 3, 4, 1, 2)
    bc = bm.astype(jnp.float32).reshape(bsz, nc, SSM_CHUNK, g, ns)
    cc = cm.astype(jnp.float32).reshape(bsz, nc, SSM_CHUNK, g, ns)
    a_cs = jnp.cumsum(da, axis=-1)
    lmat = segsum_exp(a_cs)
    cb = jnp.einsum('bclgn,bcsgn->bgcls', cc, bc)
    y_diag = jnp.einsum('bgcls,bgecls,bcsgep->bclgep', cb, lmat, x)
    decay_states = jnp.exp(a_cs[..., -1:] - a_cs)
    states = jnp.einsum('bcsgn,bgecs,bcsgep->bcgepn', bc, decay_states, x)
    init = init_state.astype(jnp.float32).reshape(bsz, 1, g, e, p, ns)
    states = jnp.concatenate([init, states], axis=1)
    totals = jnp.pad(a_cs[..., -1], ((0, 0), (0, 0), (0, 0), (1, 0)))
    chunk_decay = segsum_exp(jnp.cumsum(totals, axis=-1))
    new_states = jnp.einsum('bgezc,bcgepn->bzgepn', chunk_decay, states)
    states, final = new_states[:, :-1], new_states[:, -1]
    y_off = jnp.einsum('bclgn,bcgepn,bgecl->bclgep', cc, states, jnp.exp(a_cs))
    y = (y_diag + y_off).reshape(bsz, n, nh, p)
    return y, final.reshape(bsz, nh, p, ns)


def mixer_ssd(u_lat, u_ctx, conv_w, conv_b, dt_bias, a_log, d_skip, norm_g, with_ctx_out):
    bsz = u_lat.shape[0]
    n_heads = a_log.shape[-1]
    d_inner = n_heads * SSM_HEAD_DIM
    gn = SSM_GROUPS * SSM_D_STATE

    def prep(u):
        n = u.shape[1]
        z, xbc, dt = jnp.split(u, [d_inner, 2 * d_inner + 2 * gn], axis=-1)
        xbc = jax.nn.silu(dwconv_centred(xbc, conv_w, conv_b))
        xs, bm, cm = jnp.split(xbc, [d_inner, d_inner + gn], axis=-1)
        xs = xs.reshape(bsz, n, n_heads, SSM_HEAD_DIM)
        bm = bm.reshape(bsz, n, SSM_GROUPS, SSM_D_STATE)
        cm = cm.reshape(bsz, n, SSM_GROUPS, SSM_D_STATE)
        dt = jax.nn.softplus(dt.astype(jnp.float32).reshape(bsz, n, 2, n_heads)
                             + dt_bias.astype(jnp.float32))
        return z, xs, bm, cm, dt

    a = -jnp.exp(a_log.astype(jnp.float32))

    def bidir(xs, bm, cm, dt, s_f, s_b):
        y_f, fin_f = ssd_scan(xs, dt[:, :, 0], a[0], bm, cm, s_f)
        y_b, fin_b = ssd_scan(xs[:, ::-1], dt[:, ::-1, 1], a[1], bm[:, ::-1], cm[:, ::-1], s_b)
        return y_f + y_b[:, ::-1], fin_f, fin_b

    def out(z, xs, y):
        n = xs.shape[1]
        y = (y + xs.astype(jnp.float32) * d_skip.astype(jnp.float32)[:, None]).reshape(bsz, n, d_inner)
        return rmsnorm(y * jax.nn.silu(z.astype(jnp.float32)), norm_g).astype(z.dtype)

    zc, xc, bcm, ccm, dtc = prep(u_ctx)
    zero = jnp.zeros((bsz, n_heads, SSM_HEAD_DIM, SSM_D_STATE), jnp.float32)
    y_c, s_f, s_b = bidir(xc, bcm, ccm, dtc, zero, zero)
    zl, xl, blm, clm, dtl = prep(u_lat)
    y_l, _, _ = bidir(xl, blm, clm, dtl, s_f, s_b)
    y_ctx = out(zc, xc, y_c) if with_ctx_out else None
    return out(zl, xl, y_l), y_ctx


def mixer_na(u_lat, u_ctx, rpb, with_ctx_out):
    bsz, n_lat, d3 = u_lat.shape
    d = d3 // 3
    n_heads = d // NA_HEAD_DIM
    rows = n_lat // GRID_W
    kh = min(NA_KH, rows)
    scale = NA_HEAD_DIM ** -0.5

    def qkv(u):
        q, k, v = jnp.split(u, 3, axis=-1)
        shp = (bsz, u.shape[1], n_heads, NA_HEAD_DIM)
        return q.reshape(shp), k.reshape(shp), v.reshape(shp)

    q_l, k_l, v_l = qkv(u_lat)
    q_c, k_c, v_c = qkv(u_ctx)
    grid = (bsz, rows, GRID_W, n_heads, NA_HEAD_DIM)
    q_g = q_l.reshape(grid).swapaxes(0, 1)
    k_g = k_l.reshape(grid)
    v_g = v_l.reshape(grid)
    col = jnp.arange(GRID_W)
    c0 = jnp.clip(col - NA_KW // 2, 0, GRID_W - NA_KW)
    col_ok = (col[None, :] >= c0[:, None]) & (col[None, :] < c0[:, None] + NA_KW)
    dx_idx = jnp.clip(col[None, :] - col[:, None] + NA_KW - 1, 0, 2 * NA_KW - 2)
    rpb_cols = rpb.astype(jnp.float32)[:, :, dx_idx]

    def row_block(args):
        r, q_r = args
        r0 = jnp.clip(r - kh // 2, 0, rows - kh)
        k_band = lax.dynamic_slice_in_dim(k_g, r0, kh, axis=1)
        v_band = lax.dynamic_slice_in_dim(v_g, r0, kh, axis=1)
        dy_idx = r0 + jnp.arange(kh) - r + NA_KH - 1
        bias = rpb_cols[:, dy_idx].transpose(0, 2, 1, 3)
        s_lat = jnp.einsum('bqhd,byxhd->bhqyx', q_r, k_band).astype(jnp.float32) * scale + bias[None]
        s_lat = jnp.where(col_ok[:, None, :], s_lat, -jnp.inf)
        s_ctx = jnp.einsum('bqhd,bshd->bhqs', q_r, k_c).astype(jnp.float32) * scale
        s = jnp.concatenate([s_lat.reshape(bsz, n_heads, GRID_W, kh * GRID_W), s_ctx], axis=-1)
        p = jax.nn.softmax(s, axis=-1).astype(v_band.dtype)
        p_lat = p[..., :kh * GRID_W].reshape(bsz, n_heads, GRID_W, kh, GRID_W)
        p_ctx = p[..., kh * GRID_W:]
        return (jnp.einsum('bhqyx,byxhd->bqhd', p_lat, v_band)
                + jnp.einsum('bhqs,bshd->bqhd', p_ctx, v_c))

    o = lax.map(row_block, (jnp.arange(rows), q_g))
    y_lat = o.swapaxes(0, 1).reshape(bsz, n_lat, d)
    y_ctx = None
    if with_ctx_out:
        y_ctx = mha(q_c, k_c, v_c).reshape(bsz, -1, d)
    return y_lat, y_ctx


def kernel(x, c, ctx, c_ctx, ada_w, ada_b, norm_g, mlp_w1, mlp_w2, ga_w_in, ga_q_gain, ga_k_gain, ga_w_out, hy_w_in, hy_conv_w, hy_conv_b, hy_f_w1, hy_f_b1, hy_f_w2, hy_f_b2, hy_f_w3, hy_f_freq, hy_skip, hy_w_out, ssm_w_in, ssm_conv_w, ssm_conv_b, ssm_dt_bias, ssm_a_log, ssm_d, ssm_norm_g, ssm_w_out, na_w_in, na_rpb, na_w_out):
    bsz, n_lat, d = x.shape
    n_ctx = ctx.shape[1]
    depth = ada_w.shape[0]
    sc = jnp.concatenate([jax.nn.silu(c), jax.nn.silu(c_ctx)[None, :]], axis=0)
    x_lat = x.reshape(bsz * n_lat, d)
    x_ctx = ctx.reshape(bsz * n_ctx, d)
    for i in range(depth):
        kind, j = i % N_MIXERS, i // N_MIXERS
        with_ctx = i < depth - 1
        mod = dense_f32(sc, ada_w[i], ada_b[i][None, :]).reshape(bsz + 1, 6, d)
        mod_l, mod_c = mod[:bsz], mod[bsz:]
        g = norm_g[i]
        w_in, w_out = ((ga_w_in, ga_w_out), (hy_w_in, hy_w_out), (ssm_w_in, ssm_w_out), (na_w_in, na_w_out))[kind]
        w_in, w_out = w_in[j].astype(BF16), w_out[j].astype(BF16)
        u_l = norm_proj(x_lat, mod_l, g, w_in, n_lat).reshape(bsz, n_lat, -1)
        u_c = norm_proj(x_ctx, mod_c, g, w_in, bsz * n_ctx).reshape(bsz, n_ctx, -1)
        if kind == 0:
            a_l, a_c = mixer_gqa(u_l, u_c, ga_q_gain[j], ga_k_gain[j], with_ctx)
        elif kind == 1:
            a_l, a_c = mixer_hyena(u_l, u_c, hy_conv_w[j], hy_conv_b[j], hy_f_w1[j], hy_f_b1[j], hy_f_w2[j],
                                   hy_f_b2[j], hy_f_w3[j], hy_f_freq[j], hy_skip[j], with_ctx)
        elif kind == 2:
            a_l, a_c = mixer_ssd(u_l, u_c, ssm_conv_w[j], ssm_conv_b[j], ssm_dt_bias[j], ssm_a_log[j],
                                 ssm_d[j], ssm_norm_g[j], with_ctx)
        else:
            a_l, a_c = mixer_na(u_l, u_c, na_rpb[j], with_ctx)
        w1, w2 = mlp_w1[i].astype(BF16), mlp_w2[i].astype(BF16)
        x_lat, h_l = out_proj_residual(a_l.reshape(bsz * n_lat, -1), w_out, x_lat, mod_l, g, n_lat)
        x_lat = mlp_residual(h_l, w1, w2, x_lat, mod_l, g, n_lat)
        if with_ctx:
            x_ctx, h_c = out_proj_residual(a_c.reshape(bsz * n_ctx, -1), w_out, x_ctx, mod_c, g, bsz * n_ctx)
            x_ctx = mlp_residual(h_c, w1, w2, x_ctx, mod_c, g, bsz * n_ctx)
    return x_lat.reshape(bsz, n_lat, d)
```

```python
import functools
import math

import jax
import jax.numpy as jnp
from jax import lax
from jax.experimental import pallas as pl
from jax.experimental.pallas import tpu as pltpu

GRID_W = 64
N_MIXERS = 4
RMS_EPS = 1e-6

GA_HEAD_DIM = 128
GA_N_KV = 2
ROPE_THETA = 10000.0

HY_ORDER = 2
HY_BANDS = 16
HY_DECAY_SHORT = 0.3
HY_DECAY_LONG = 1.5
HY_TARGET = 1e-2
HY_TILE = 128
HY_CH_BLOCK = 8

SSM_HEAD_DIM = 64
SSM_GROUPS = 4
SSM_D_STATE = 128
SSM_CHUNK = 128

NA_HEAD_DIM = 64
NA_KH = 8
NA_KW = 16

VMEM_LIMIT_BYTES = 56 * 1024 * 1024
BF16 = jnp.bfloat16
F32 = jnp.float32
HIGHEST = lax.Precision.HIGHEST
NT_DIMS = (((1,), (1,)), ((), ()))


def _params(n_grid_dims):
    return pltpu.CompilerParams(dimension_semantics=("arbitrary",) * n_grid_dims,
                                vmem_limit_bytes=VMEM_LIMIT_BYTES)


def _row_tile(rows_per_mod, cap):
    t = min(rows_per_mod, cap)
    while rows_per_mod % t:
        t //= 2
    return t


def _col_chunks(n, width):
    return [(c, min(width, n - c)) for c in range(0, n, width)]


def _rms(x, g):
    return x * lax.rsqrt(jnp.mean(x * x, axis=-1, keepdims=True) + RMS_EPS) * g


def _silu(x):
    return x / (1.0 + jnp.exp(-x))


def _dense_kernel(a_ref, w_ref, b_ref, o_ref):
    o_ref[...] = jnp.dot(a_ref[...], w_ref[...], preferred_element_type=F32, precision=HIGHEST) + b_ref[...]


def dense_f32(a, w, b):
    m, k = a.shape
    n = w.shape[1]
    tn = 1024 if n % 1024 == 0 else n
    return pl.pallas_call(
        _dense_kernel,
        grid=(n // tn,),
        in_specs=[pl.BlockSpec((m, k), lambda j: (0, 0)),
                  pl.BlockSpec((k, tn), lambda j: (0, j)),
                  pl.BlockSpec((1, tn), lambda j: (0, j))],
        out_specs=pl.BlockSpec((m, tn), lambda j: (0, j)),
        out_shape=jax.ShapeDtypeStruct((m, n), F32),
        compiler_params=_params(1),
        name="ada_dense",
    )(a, w, b)


def _norm_proj_kernel(x_ref, mod_ref, g_ref, w_ref, o_ref, *, tn):
    x = x_ref[...]
    h = _rms(x, g_ref[0:1, :]) * (1.0 + mod_ref[0, 1:2, :]) + mod_ref[0, 0:1, :]
    h = h.astype(BF16)
    for c, s in _col_chunks(w_ref.shape[1], tn):
        o_ref[:, c:c + s] = jnp.dot(h, w_ref[:, c:c + s], preferred_element_type=F32).astype(o_ref.dtype)


def norm_proj(x, mod, g, w, rows_per_mod, out_dtype=F32):
    m, d = x.shape
    n = w.shape[1]
    tm = _row_tile(rows_per_mod, 512)
    bpm = rows_per_mod // tm
    return pl.pallas_call(
        functools.partial(_norm_proj_kernel, tn=512),
        grid=(m // tm,),
        in_specs=[pl.BlockSpec((tm, d), lambda i: (i, 0)),
                  pl.BlockSpec((1, 6, d), lambda i: (i // bpm, 0, 0)),
                  pl.BlockSpec((4, d), lambda i: (0, 0)),
                  pl.BlockSpec((d, n), lambda i: (0, 0))],
        out_specs=pl.BlockSpec((tm, n), lambda i: (i, 0)),
        out_shape=jax.ShapeDtypeStruct((m, n), out_dtype),
        compiler_params=_params(1),
        name="norm_proj",
    )(x, mod, g, w)


def _out_proj_kernel(a_ref, w_ref, x_ref, mod_ref, g_ref, xo_ref, h_ref):
    y = jnp.dot(a_ref[...].astype(BF16), w_ref[...], preferred_element_type=F32)
    x = x_ref[...] + mod_ref[0, 2:3, :] * _rms(y, g_ref[1:2, :])
    xo_ref[...] = x
    h = _rms(x, g_ref[2:3, :]) * (1.0 + mod_ref[0, 4:5, :]) + mod_ref[0, 3:4, :]
    h_ref[...] = h.astype(BF16)


def out_proj_residual(a, w, x, mod, g, rows_per_mod):
    m, k = a.shape
    d = w.shape[1]
    tm = _row_tile(rows_per_mod, 512)
    bpm = rows_per_mod // tm
    return pl.pallas_call(
        _out_proj_kernel,
        grid=(m // tm,),
        in_specs=[pl.BlockSpec((tm, k), lambda i: (i, 0)),
                  pl.BlockSpec((k, d), lambda i: (0, 0)),
                  pl.BlockSpec((tm, d), lambda i: (i, 0)),
                  pl.BlockSpec((1, 6, d), lambda i: (i // bpm, 0, 0)),
                  pl.BlockSpec((4, d), lambda i: (0, 0))],
        out_specs=[pl.BlockSpec((tm, d), lambda i: (i, 0)),
                   pl.BlockSpec((tm, d), lambda i: (i, 0))],
        out_shape=[jax.ShapeDtypeStruct((m, d), F32), jax.ShapeDtypeStruct((m, d), BF16)],
        compiler_params=_params(1),
        name="out_proj",
    )(a, w, x, mod, g)


def _mlp_kernel(h_ref, w1_ref, w2_ref, x_ref, mod_ref, g_ref, o_ref, *, fc):
    h = h_ref[...]
    acc = jnp.zeros(o_ref.shape, F32)
    for c, s in _col_chunks(w1_ref.shape[1], fc):
        u = jnp.dot(h, w1_ref[:, c:c + s], preferred_element_type=F32)
        u = jnp.square(jnp.maximum(u, 0.0)).astype(BF16)
        acc = acc + jnp.dot(u, w2_ref[c:c + s, :], preferred_element_type=F32)
    o_ref[...] = x_ref[...] + mod_ref[0, 5:6, :] * _rms(acc, g_ref[3:4, :])


def mlp_residual(h, w1, w2, x, mod, g, rows_per_mod):
    m, d = x.shape
    f = w1.shape[1]
    tm = _row_tile(rows_per_mod, 512)
    bpm = rows_per_mod // tm
    return pl.pallas_call(
        functools.partial(_mlp_kernel, fc=1024),
        grid=(m // tm,),
        in_specs=[pl.BlockSpec((tm, d), lambda i: (i, 0)),
                  pl.BlockSpec((d, f), lambda i: (0, 0)),
                  pl.BlockSpec((f, d), lambda i: (0, 0)),
                  pl.BlockSpec((tm, d), lambda i: (i, 0)),
                  pl.BlockSpec((1, 6, d), lambda i: (i // bpm, 0, 0)),
                  pl.BlockSpec((4, d), lambda i: (0, 0))],
        out_specs=pl.BlockSpec((tm, d), lambda i: (i, 0)),
        out_shape=jax.ShapeDtypeStruct((m, d), F32),
        compiler_params=_params(1),
        name="mlp",
    )(h, w1, w2, x, mod, g)


def _rope(x, cos, sin_lo, sin_hi):
    return x * cos + pltpu.roll(x, 96, 1) * sin_lo + pltpu.roll(x, 32, 1) * sin_hi


def _gqa_kernel(*refs, n_lat, group):
    hd = GA_HEAD_DIM
    if n_lat:
        (q_ref, kl_ref, vl_ref, kc_ref, vc_ref, qg_ref, kg_ref, tq_ref, tk_ref, o_ref, k_scr, v_scr) = refs
    else:
        (q_ref, kc_ref, vc_ref, qg_ref, kg_ref, o_ref, k_scr, v_scr) = refs

    @pl.when(pl.program_id(2) == 0)
    def _():
        if n_lat:
            k = _rope(_rms(kl_ref[0], kg_ref[...]), tk_ref[0], tk_ref[1], tk_ref[2])
            k_scr[0:n_lat, :] = k.astype(BF16)
            v_scr[0:n_lat, :] = vl_ref[0].astype(BF16)
        k_scr[n_lat:, :] = _rms(kc_ref[0], kg_ref[...]).astype(BF16)
        v_scr[n_lat:, :] = vc_ref[0].astype(BF16)

    for h in range(group):
        q = _rms(q_ref[0, :, h * hd:(h + 1) * hd], qg_ref[...])
        if n_lat:
            q = _rope(q, tq_ref[0], tq_ref[1], tq_ref[2])
        q = q.astype(BF16)
        s = lax.dot_general(q, k_scr[...], NT_DIMS, preferred_element_type=F32) * (hd ** -0.5)
        p = jnp.exp(s - jnp.max(s, axis=-1, keepdims=True))
        l = jnp.sum(p, axis=-1, keepdims=True)
        o = jnp.dot(p.astype(BF16), v_scr[...], preferred_element_type=F32) / l
        o_ref[0, :, h * hd:(h + 1) * hd] = o.astype(o_ref.dtype)


def _rope_tables(n_tokens, head_dim):
    t = jnp.arange(n_tokens)
    row = (t // GRID_W).astype(F32)
    col = (t % GRID_W).astype(F32)
    nf = head_dim // 4
    inv = ROPE_THETA ** (-jnp.arange(nf, dtype=F32) / nf)
    ar = row[:, None] * inv[None, :]
    ac = col[:, None] * inv[None, :]
    zero = jnp.zeros_like(ar)
    cos = jnp.concatenate([jnp.cos(ar), jnp.cos(ar), jnp.cos(ac), jnp.cos(ac)], axis=-1)
    sin_lo = jnp.concatenate([-jnp.sin(ar), zero, -jnp.sin(ac), zero], axis=-1)
    sin_hi = jnp.concatenate([zero, jnp.sin(ar), zero, jnp.sin(ac)], axis=-1)
    return jnp.stack([cos, sin_lo, sin_hi])


def gqa_attention(u_q, u_lat, u_ctx, q_gain, k_gain):
    hd = GA_HEAD_DIM
    bsz, nq, width = u_q.shape
    n_heads = width // hd - 2 * GA_N_KV
    group = n_heads // GA_N_KV
    n_lat = 0 if u_lat is None else u_lat.shape[1]
    n_ctx = u_ctx.shape[1]
    tq = _row_tile(nq, 256)
    k_col, v_col = n_heads, n_heads + GA_N_KV
    qg, kg = q_gain.reshape(1, hd), k_gain.reshape(1, hd)
    q_spec = pl.BlockSpec((1, tq, group * hd), lambda b, kv, i: (b, i, kv))
    ctx_specs = [pl.BlockSpec((1, n_ctx, hd), lambda b, kv, i: (b, 0, k_col + kv)),
                 pl.BlockSpec((1, n_ctx, hd), lambda b, kv, i: (b, 0, v_col + kv))]
    gain_specs = [pl.BlockSpec((1, hd), lambda b, kv, i: (0, 0))] * 2
    if n_lat:
        tables = _rope_tables(n_lat, hd)
        in_specs = ([q_spec,
                     pl.BlockSpec((1, n_lat, hd), lambda b, kv, i: (b, 0, k_col + kv)),
                     pl.BlockSpec((1, n_lat, hd), lambda b, kv, i: (b, 0, v_col + kv))]
                    + ctx_specs + gain_specs
                    + [pl.BlockSpec((3, tq, hd), lambda b, kv, i: (0, i, 0)),
                       pl.BlockSpec((3, n_lat, hd), lambda b, kv, i: (0, 0, 0))])
        args = (u_q, u_lat, u_lat, u_ctx, u_ctx, qg, kg, tables, tables)
    else:
        in_specs = [q_spec] + ctx_specs + gain_specs
        args = (u_q, u_ctx, u_ctx, qg, kg)
    return pl.pallas_call(
        functools.partial(_gqa_kernel, n_lat=n_lat, group=group),
        grid=(bsz, GA_N_KV, nq // tq),
        in_specs=in_specs,
        out_specs=pl.BlockSpec((1, tq, group * hd), lambda b, kv, i: (b, i, kv)),
        out_shape=jax.ShapeDtypeStruct((bsz, nq, n_heads * hd), BF16),
        scratch_shapes=[pltpu.VMEM((n_lat + n_ctx, hd), BF16), pltpu.VMEM((n_lat + n_ctx, hd), BF16)],
        compiler_params=_params(3),
        name="gqa_lat" if n_lat else "gqa_ctx",
    )(*args)


def _dwconv_kernel(u_ref, w_ref, b_ref, o_ref, *, silu, transpose_out):
    x = u_ref[0]
    n = x.shape[0]
    row = lax.broadcasted_iota(jnp.int32, x.shape, 0)
    prev = jnp.where(row == 0, 0.0, pltpu.roll(x, 1, 0))
    nxt = jnp.where(row == n - 1, 0.0, pltpu.roll(x, n - 1, 0))
    y = prev * w_ref[0:1, :] + x * w_ref[1:2, :] + nxt * w_ref[2:3, :] + b_ref[...]
    if silu:
        y = _silu(y)
    if transpose_out:
        o_ref[...] = y.T.astype(o_ref.dtype)
    else:
        o_ref[0] = y.astype(o_ref.dtype)


def dwconv3(u, col0, w, b, silu, transpose_out=False):
    bsz, n, _ = u.shape
    ch = w.shape[1]
    tc = 512
    assert ch % tc == 0 and col0 % tc == 0 and w.shape[0] == 3
    if transpose_out:
        out_spec = pl.BlockSpec((tc, n), lambda b_, j: (j, b_))
        out_shape = jax.ShapeDtypeStruct((ch, bsz * n), F32)
    else:
        out_spec = pl.BlockSpec((1, n, tc), lambda b_, j: (b_, 0, j))
        out_shape = jax.ShapeDtypeStruct((bsz, n, ch), F32)
    out = pl.pallas_call(
        functools.partial(_dwconv_kernel, silu=silu, transpose_out=transpose_out),
        grid=(bsz, ch // tc),
        in_specs=[pl.BlockSpec((1, n, tc), lambda b_, j: (b_, 0, col0 // tc + j)),
                  pl.BlockSpec((3, tc), lambda b_, j: (0, j)),
                  pl.BlockSpec((1, tc), lambda b_, j: (0, j))],
        out_specs=out_spec,
        out_shape=out_shape,
        compiler_params=_params(2),
        name="dwconv3",
    )(u, w, b.reshape(1, ch))
    return out.reshape(ch, bsz, n) if transpose_out else out


def _hyena_filter_kernel(feats_ref, w1_ref, b1_ref, w2_ref, b2_ref, fr_ref, w3_ref, dl_ref, o_ref, h_scr):
    @pl.when(pl.program_id(0) == 0)
    def _():
        h = jnp.dot(w1_ref[...], feats_ref[...], precision=HIGHEST, preferred_element_type=F32) + b1_ref[...]
        h = jnp.sin(fr_ref[:, 0:1] * h)
        h = jnp.dot(w2_ref[...], h, precision=HIGHEST, preferred_element_type=F32) + b2_ref[...]
        h_scr[...] = jnp.sin(fr_ref[:, 1:2] * h)

    h = jnp.dot(w3_ref[...], h_scr[...], precision=HIGHEST, preferred_element_type=F32)
    o_ref[...] = h * jnp.exp(-dl_ref[...] * feats_ref[0:1, :])


def hyena_filters(n, d, w1, b1, w2, b2, w3, freq):
    t = jnp.linspace(0.0, 1.0, n, dtype=F32)[None, :]
    w = 2.0 * math.pi * jnp.arange(n, dtype=F32)[None, :] / n
    bands = jnp.linspace(1e-4, HY_BANDS - 1, HY_BANDS, dtype=F32)[:, None]
    feats = jnp.concatenate([t, jnp.cos(bands * w), -jnp.sin(bands * w)], axis=0)
    deltas = jnp.abs(jnp.linspace(math.log(HY_TARGET) / HY_DECAY_SHORT,
                                  math.log(HY_TARGET) / HY_DECAY_LONG, d, dtype=F32))
    n_f = 2 * HY_ORDER * d
    dl = jnp.tile(deltas, 2 * HY_ORDER).reshape(n_f, 1)
    hid = w1.shape[1]
    emb = -(-feats.shape[0] // 8) * 8
    feats = jnp.pad(feats, ((0, emb - feats.shape[0]), (0, 0)))
    w1 = jnp.pad(w1, ((0, emb - w1.shape[0]), (0, 0)))
    tr = 512
    assert n_f % tr == 0
    const = lambda shape: pl.BlockSpec(shape, lambda j: (0, 0))
    return pl.pallas_call(
        _hyena_filter_kernel,
        grid=(n_f // tr,),
        in_specs=[const((emb, n)), const((hid, emb)), const((hid, 1)), const((hid, hid)), const((hid, 1)),
                  const((hid, 2)),
                  pl.BlockSpec((tr, hid), lambda j: (j, 0)),
                  pl.BlockSpec((tr, 1), lambda j: (j, 0))],
        out_specs=pl.BlockSpec((tr, n), lambda j: (j, 0)),
        out_shape=jax.ShapeDtypeStruct((n_f, n), F32),
        scratch_shapes=[pltpu.VMEM((hid, n), F32)],
        compiler_params=_params(1),
        name="hyena_filter",
    )(feats, w1.T, b1.reshape(hid, 1), w2.T, b2.reshape(hid, 1), freq.T, w3.T, dl)


def _toeplitz_conv(z, gline):
    n = z.shape[1]
    tt = min(HY_TILE, n)
    g = pltpu.roll(jnp.broadcast_to(gline, (tt, 2 * n)), 0, 1, stride=1, stride_axis=0).astype(BF16)
    zb = z.astype(BF16)
    acc = jnp.zeros(z.shape, F32)
    for j in range(n // tt):
        acc = acc + jnp.dot(zb[:, j * tt:(j + 1) * tt], g[:, n - j * tt:2 * n - j * tt],
                            preferred_element_type=F32)
    return acc


def _hyena_conv_kernel(v_ref, x1_ref, x2_ref, g1_ref, g2_ref, sk_ref, o_ref):
    def body(c, carry):
        v = v_ref[c]
        z = x1_ref[c] * (_toeplitz_conv(v, g1_ref[pl.ds(c, 1), :]) + v * sk_ref[pl.ds(c, 1), 0:1])
        z = x2_ref[c] * (_toeplitz_conv(z, g2_ref[pl.ds(c, 1), :]) + z * sk_ref[pl.ds(c, 1), 1:2])
        o_ref[c] = z.astype(o_ref.dtype)
        return carry

    lax.fori_loop(0, v_ref.shape[0], body, 0)


def hyena_mixer(u, conv_w, conv_b, f_w1, f_b1, f_w2, f_b2, f_w3, f_freq, skip):
    bsz, n, d3 = u.shape
    d = d3 // (HY_ORDER + 1)
    dc = HY_CH_BLOCK
    assert d % dc == 0 and n % min(HY_TILE, n) == 0
    uc = dwconv3(u, 0, conv_w, conv_b, silu=False, transpose_out=True)
    filt = hyena_filters(n, d, f_w1, f_b1, f_w2, f_b2, f_w3, f_freq).reshape(2 * HY_ORDER, d, n)

    def gline(fwd, bwd):
        return jnp.concatenate([jnp.zeros((d, 1), F32), jnp.flip(bwd[:, 1:], axis=1), fwd], axis=1)

    g1, g2 = gline(filt[0], filt[1]), gline(filt[2], filt[3])
    nb = d // dc
    z = pl.pallas_call(
        _hyena_conv_kernel,
        grid=(nb,),
        in_specs=[pl.BlockSpec((dc, bsz, n), lambda j: (j, 0, 0)),
                  pl.BlockSpec((dc, bsz, n), lambda j: (nb + j, 0, 0)),
                  pl.BlockSpec((dc, bsz, n), lambda j: (2 * nb + j, 0, 0)),
                  pl.BlockSpec((dc, 2 * n), lambda j: (j, 0)),
                  pl.BlockSpec((dc, 2 * n), lambda j: (j, 0)),
                  pl.BlockSpec((dc, HY_ORDER), lambda j: (j, 0))],
        out_specs=pl.BlockSpec((dc, bsz, n), lambda j: (j, 0, 0)),
        out_shape=jax.ShapeDtypeStruct((d, bsz, n), BF16),
        compiler_params=_params(1),
        name="hyena_conv",
    )(uc, uc, uc, g1, g2, skip.T)
    return z.transpose(1, 2, 0)


def _softplus(x):
    return jnp.maximum(x, 0.0) + jnp.log(1.0 + jnp.exp(-jnp.abs(x)))


def _ssd_kernel(*refs, reverse, epilogue, heads_per_group):
    hp_dim, ns, q = SSM_HEAD_DIM, SSM_D_STATE, SSM_CHUNK
    if epilogue:
        (xs_ref, bc_ref, dt_ref, dtt_ref, dtb_ref, dtbt_ref, al_ref, alt_ref, s0_ref,
         yp_ref, z_ref, dsk_ref, ng_ref, y_ref, sf_ref, s_scr) = refs
    else:
        (xs_ref, bc_ref, dt_ref, dtt_ref, dtb_ref, dtbt_ref, al_ref, alt_ref, s0_ref,
         y_ref, sf_ref, s_scr) = refs
    c = pl.program_id(1)
    n_groups = s_scr.shape[0]
    n_heads = n_groups * heads_per_group
    gw = heads_per_group * hp_dim

    @pl.when(c == 0)
    def _():
        s_scr[...] = s0_ref[0]

    li = lax.broadcasted_iota(jnp.int32, (q, q), 0)
    si = lax.broadcasted_iota(jnp.int32, (q, q), 1)
    keep = (si >= li) if reverse else (si <= li)
    tri = keep.astype(F32)
    tri_t = ((li >= si) if reverse else (li <= si)).astype(F32)
    edge = 0 if reverse else q - 1

    dt = _softplus(dt_ref[0] + dtb_ref[...])
    da = dt * -jnp.exp(al_ref[...])
    a_cs = jnp.dot(tri, da, precision=HIGHEST, preferred_element_type=F32)
    da_t = _softplus(dtt_ref[0] + dtbt_ref[...]) * -jnp.exp(alt_ref[...])
    a_cs_t = jnp.dot(da_t, tri_t, precision=HIGHEST, preferred_element_type=F32)

    eh = lax.broadcasted_iota(jnp.int32, (n_heads, n_heads * hp_dim), 0)
    el = lax.broadcasted_iota(jnp.int32, (n_heads, n_heads * hp_dim), 1)
    expand_p = (el // hp_dim == eh).astype(F32)
    eh2 = lax.broadcasted_iota(jnp.int32, (n_heads, n_heads * q), 0)
    el2 = lax.broadcasted_iota(jnp.int32, (n_heads, n_heads * q), 1)
    expand_q = (el2 // q == eh2).astype(F32)
    dt_full = jnp.dot(dt, expand_p, precision=HIGHEST, preferred_element_type=F32)
    acs_full = jnp.dot(a_cs, expand_p, precision=HIGHEST, preferred_element_type=F32)
    acs_tile = jnp.dot(a_cs, expand_q, precision=HIGHEST, preferred_element_type=F32)
    tot_full = acs_full[edge:edge + 1, :]

    xs = xs_ref[0]
    x_dt = xs * dt_full
    x_in = x_dt.astype(BF16)
    x_st = (x_dt * jnp.exp(tot_full - acs_full)).astype(BF16)
    exp_a = jnp.exp(acs_full)
    lane = lax.broadcasted_iota(jnp.int32, (q, 2 * hp_dim), 1)
    first = lane < hp_dim
    gn = n_groups * ns
    ys = []
    for g in range(n_groups):
        b_g = bc_ref[0, :, g * ns:(g + 1) * ns]
        c_g = bc_ref[0, :, gn + g * ns:gn + (g + 1) * ns].astype(BF16)
        cb = lax.dot_general(c_g, b_g.astype(BF16), NT_DIMS, preferred_element_type=F32)
        s_prev = s_scr[g]
        y_off = jnp.dot(c_g, s_prev.astype(BF16), preferred_element_type=F32)
        gl = slice(g * gw, (g + 1) * gw)
        s_new = jnp.dot(b_g.T.astype(BF16), x_st[:, gl], preferred_element_type=F32)
        s_scr[g] = jnp.exp(tot_full[:, gl]) * s_prev + s_new
        for e in range(0, heads_per_group, 2):
            ms = []
            for h in (g * heads_per_group + e, g * heads_per_group + e + 1):
                diff = acs_tile[:, h * q:(h + 1) * q] - a_cs_t[h:h + 1, :]
                lmat = jnp.where(keep, jnp.exp(jnp.minimum(diff, 0.0)), 0.0)
                ms.append((cb * lmat).astype(BF16))
            pl_ = slice(g * gw + e * hp_dim, g * gw + (e + 2) * hp_dim)
            yd = jnp.dot(jnp.concatenate(ms, axis=0), x_in[:, pl_], preferred_element_type=F32)
            yd = jnp.where(first, yd[:q], yd[q:])
            y = yd + y_off[:, e * hp_dim:(e + 2) * hp_dim] * exp_a[:, pl_]
            if epilogue:
                y = y + yp_ref[0, :, pl_] + xs[:, pl_] * dsk_ref[:, pl_]
                ys.append(y * _silu(z_ref[0, :, pl_]))
            else:
                y_ref[0, :, pl_] = y

    if epilogue:
        y_ref[0] = _rms(jnp.concatenate(ys, axis=1), ng_ref[...]).astype(y_ref.dtype)

    @pl.when(c == pl.num_programs(1) - 1)
    def _():
        sf_ref[0] = s_scr[...]


def ssd_scan_dir(xbc, dt, dt_bias, a_log, s0, reverse, epi=None):
    bsz, n, _ = xbc.shape
    n_heads = dt.shape[-1]
    n_groups = SSM_GROUPS
    hpg = n_heads // n_groups
    d_inner = n_heads * SSM_HEAD_DIM
    gn = n_groups * SSM_D_STATE
    q = SSM_CHUNK
    nc = n // q
    assert d_inner % (2 * gn) == 0 and hpg % 2 == 0 and n % q == 0
    cix = (lambda c: nc - 1 - c) if reverse else (lambda c: c)
    dt_t = dt.swapaxes(1, 2)
    in_specs = [pl.BlockSpec((1, q, d_inner), lambda b, c: (b, cix(c), 0)),
                pl.BlockSpec((1, q, 2 * gn), lambda b, c: (b, cix(c), d_inner // (2 * gn))),
                pl.BlockSpec((1, q, n_heads), lambda b, c: (b, cix(c), 0)),
                pl.BlockSpec((1, n_heads, q), lambda b, c: (b, 0, cix(c))),
                pl.BlockSpec((1, n_heads), lambda b, c: (0, 0)),
                pl.BlockSpec((n_heads, 1), lambda b, c: (0, 0)),
                pl.BlockSpec((1, n_heads), lambda b, c: (0, 0)),
                pl.BlockSpec((n_heads, 1), lambda b, c: (0, 0)),
                pl.BlockSpec((1,) + s0.shape[1:], lambda b, c: (b, 0, 0, 0))]
    args = [xbc, xbc, dt, dt_t, dt_bias.reshape(1, n_heads), dt_bias.reshape(n_heads, 1),
            a_log.reshape(1, n_heads), a_log.reshape(n_heads, 1), s0]
    y_dtype = F32
    if epi is not None:
        y_prev, u, d_skip, norm_g = epi
        in_specs += [pl.BlockSpec((1, q, d_inner), lambda b, c: (b, cix(c), 0)),
                     pl.BlockSpec((1, q, d_inner), lambda b, c: (b, cix(c), 0)),
                     pl.BlockSpec((1, d_inner), lambda b, c: (0, 0)),
                     pl.BlockSpec((1, d_inner), lambda b, c: (0, 0))]
        args += [y_prev, u, jnp.repeat(d_skip, SSM_HEAD_DIM).reshape(1, d_inner), norm_g.reshape(1, d_inner)]
        y_dtype = BF16
    return pl.pallas_call(
        functools.partial(_ssd_kernel, reverse=reverse, epilogue=epi is not None, heads_per_group=hpg),
        grid=(bsz, nc),
        in_specs=in_specs,
        out_specs=[pl.BlockSpec((1, q, d_inner), lambda b, c: (b, cix(c), 0)),
                   pl.BlockSpec((1,) + s0.shape[1:], lambda b, c: (b, 0, 0, 0))],
        out_shape=[jax.ShapeDtypeStruct((bsz, n, d_inner), y_dtype), jax.ShapeDtypeStruct(s0.shape, F32)],
        scratch_shapes=[pltpu.VMEM(s0.shape[1:], F32)],
        compiler_params=_params(2),
        name="ssd_bwd" if reverse else "ssd_fwd",
    )(*args)


def ssd_mixer(u_lat, u_ctx, conv_w, conv_b, dt_bias, a_log, d_skip, norm_g, with_ctx_out):
    bsz = u_lat.shape[0]
    n_heads = a_log.shape[-1]
    d_inner = n_heads * SSM_HEAD_DIM
    gn = SSM_GROUPS * SSM_D_STATE
    dt0 = 2 * d_inner + 2 * gn
    xbc_c = dwconv3(u_ctx, d_inner, conv_w, conv_b, silu=True)
    xbc_l = dwconv3(u_lat, d_inner, conv_w, conv_b, silu=True)
    zero = jnp.zeros((bsz, SSM_GROUPS, SSM_D_STATE, d_inner // SSM_GROUPS), F32)

    def dts(u, d):
        return u[:, :, dt0 + d * n_heads:dt0 + (d + 1) * n_heads]

    epi_args = (d_skip, norm_g)
    y_cf, s_f = ssd_scan_dir(xbc_c, dts(u_ctx, 0), dt_bias[0], a_log[0], zero, False)
    y_c, s_b = ssd_scan_dir(xbc_c, dts(u_ctx, 1), dt_bias[1], a_log[1], zero, True, (y_cf, u_ctx) + epi_args)
    y_lf, _ = ssd_scan_dir(xbc_l, dts(u_lat, 0), dt_bias[0], a_log[0], s_f, False)
    y_l, _ = ssd_scan_dir(xbc_l, dts(u_lat, 1), dt_bias[1], a_log[1], s_b, True, (y_lf, u_lat) + epi_args)
    return y_l, (y_c if with_ctx_out else None)


NEG_BIG = -1e30


def _split_heads(q, first):
    zero = jnp.zeros_like(q)
    return jnp.concatenate([jnp.where(first, q, zero), jnp.where(first, zero, q)], axis=0)


def _na_kernel(q_ref, k_ref, v_ref, kc_ref, vc_ref, bias_ref, o_ref, *, rows, n_pairs):
    r = pl.program_id(1)
    r0 = jnp.clip(r - NA_KH // 2, 0, rows - NA_KH)
    start = pl.multiple_of(r0 * GRID_W, GRID_W)
    band = NA_KH * GRID_W
    scale = NA_HEAD_DIM ** -0.5
    lane = lax.broadcasted_iota(jnp.int32, (GRID_W, 2 * NA_HEAD_DIM), 1)
    first = lane < NA_HEAD_DIM
    for hp in range(n_pairs):
        cols = slice(hp * 2 * NA_HEAD_DIM, (hp + 1) * 2 * NA_HEAD_DIM)
        q2 = _split_heads(q_ref[0, :, cols], first)
        s_lat = lax.dot_general(q2, k_ref[0, pl.ds(start, band), cols], NT_DIMS, preferred_element_type=F32)
        s_lat = s_lat * scale + bias_ref[0, hp]
        s_ctx = lax.dot_general(q2, kc_ref[0, :, cols], NT_DIMS, preferred_element_type=F32) * scale
        m = jnp.maximum(jnp.max(s_lat, axis=-1, keepdims=True), jnp.max(s_ctx, axis=-1, keepdims=True))
        p_lat = jnp.exp(s_lat - m)
        p_ctx = jnp.exp(s_ctx - m)
        l = jnp.sum(p_lat, axis=-1, keepdims=True) + jnp.sum(p_ctx, axis=-1, keepdims=True)
        o = (jnp.dot(p_lat.astype(BF16), v_ref[0, pl.ds(start, band), cols], preferred_element_type=F32)
             + jnp.dot(p_ctx.astype(BF16), vc_ref[0, :, cols], preferred_element_type=F32)) / l
        o_ref[0, :, cols] = jnp.where(first, o[:GRID_W], o[GRID_W:]).astype(o_ref.dtype)


def _na_bias(rpb):
    n_heads = rpb.shape[0]
    col = jnp.arange(GRID_W)
    c0 = jnp.clip(col - NA_KW // 2, 0, GRID_W - NA_KW)
    col_ok = (col[None, :] >= c0[:, None]) & (col[None, :] < c0[:, None] + NA_KW)
    dx_idx = jnp.clip(col[None, :] - col[:, None] + NA_KW - 1, 0, 2 * NA_KW - 2)
    rpb_cols = rpb.astype(F32)[:, :, dx_idx]
    dy_idx = jnp.arange(NA_KH)[None, :] - jnp.arange(NA_KH)[:, None] + NA_KH - 1
    bias = rpb_cols[:, dy_idx]
    bias = jnp.where(col_ok[None, None, None], bias, NEG_BIG)
    return bias.transpose(1, 0, 3, 2, 4).reshape(NA_KH, n_heads // 2, 2 * GRID_W, NA_KH * GRID_W)


def na_attention(u_lat, u_ctx, rpb):
    bsz, n_lat, d3 = u_lat.shape
    d = d3 // 3
    n_ctx = u_ctx.shape[1]
    rows = n_lat // GRID_W
    assert rows >= NA_KH and d % (2 * NA_HEAD_DIM) == 0
    n_pairs = d // (2 * NA_HEAD_DIM)
    bias = _na_bias(rpb)

    def variant(r):
        return r - jnp.clip(r - NA_KH // 2, 0, rows - NA_KH)

    return pl.pallas_call(
        functools.partial(_na_kernel, rows=rows, n_pairs=n_pairs),
        grid=(bsz, rows),
        in_specs=[pl.BlockSpec((1, GRID_W, d), lambda b, r: (b, r, 0)),
                  pl.BlockSpec((1, n_lat, d), lambda b, r: (b, 0, 1)),
                  pl.BlockSpec((1, n_lat, d), lambda b, r: (b, 0, 2)),
                  pl.BlockSpec((1, n_ctx, d), lambda b, r: (b, 0, 1)),
                  pl.BlockSpec((1, n_ctx, d), lambda b, r: (b, 0, 2)),
                  pl.BlockSpec((1, n_pairs, 2 * GRID_W, NA_KH * GRID_W), lambda b, r: (variant(r), 0, 0, 0))],
        out_specs=pl.BlockSpec((1, GRID_W, d), lambda b, r: (b, r, 0)),
        out_shape=jax.ShapeDtypeStruct((bsz, n_lat, d), BF16),
        compiler_params=_params(2),
        name="na_attention",
    )(u_lat, u_lat, u_lat, u_ctx, u_ctx, bias)


def _mha_ctx_kernel(q_ref, k_ref, v_ref, o_ref, *, n_pairs):
    n = q_ref.shape[1]
    scale = NA_HEAD_DIM ** -0.5
    lane = lax.broadcasted_iota(jnp.int32, (n, 2 * NA_HEAD_DIM), 1)
    first = lane < NA_HEAD_DIM
    for hp in range(n_pairs):
        cols = slice(hp * 2 * NA_HEAD_DIM, (hp + 1) * 2 * NA_HEAD_DIM)
        q2 = _split_heads(q_ref[0, :, cols], first)
        s = lax.dot_general(q2, k_ref[0, :, cols], NT_DIMS, preferred_element_type=F32) * scale
        p = jnp.exp(s - jnp.max(s, axis=-1, keepdims=True))
        l = jnp.sum(p, axis=-1, keepdims=True)
        o = jnp.dot(p.astype(BF16), v_ref[0, :, cols], preferred_element_type=F32) / l
        o_ref[0, :, cols] = jnp.where(first, o[:n], o[n:]).astype(o_ref.dtype)


def mha_ctx(u_ctx):
    bsz, n, d3 = u_ctx.shape
    d = d3 // 3
    n_pairs = d // (2 * NA_HEAD_DIM)
    return pl.pallas_call(
        functools.partial(_mha_ctx_kernel, n_pairs=n_pairs),
        grid=(bsz,),
        in_specs=[pl.BlockSpec((1, n, d), lambda b: (b, 0, 0)),
                  pl.BlockSpec((1, n, d), lambda b: (b, 0, 1)),
                  pl.BlockSpec((1, n, d), lambda b: (b, 0, 2))],
        out_specs=pl.BlockSpec((1, n, d), lambda b: (b, 0, 0)),
        out_shape=jax.ShapeDtypeStruct((bsz, n, d), BF16),
        compiler_params=_params(1),
        name="mha_ctx",
    )(u_ctx, u_ctx, u_ctx)


def kernel(x, c, ctx, c_ctx, ada_w, ada_b, norm_g, mlp_w1, mlp_w2, ga_w_in, ga_q_gain, ga_k_gain, ga_w_out, hy_w_in, hy_conv_w, hy_conv_b, hy_f_w1, hy_f_b1, hy_f_w2, hy_f_b2, hy_f_w3, hy_f_freq, hy_skip, hy_w_out, ssm_w_in, ssm_conv_w, ssm_conv_b, ssm_dt_bias, ssm_a_log, ssm_d, ssm_norm_g, ssm_w_out, na_w_in, na_rpb, na_w_out):
    bsz, n_lat, d = x.shape
    n_ctx = ctx.shape[1]
    depth = ada_w.shape[0]
    sc = jnp.concatenate([jax.nn.silu(c), jax.nn.silu(c_ctx)[None, :]], axis=0)
    x_lat = x.reshape(bsz * n_lat, d)
    x_ctx = ctx.reshape(bsz * n_ctx, d)
    for i in range(depth):
        kind, j = i % N_MIXERS, i // N_MIXERS
        with_ctx = i < depth - 1
        mod = dense_f32(sc, ada_w[i], ada_b[i][None, :]).reshape(bsz + 1, 6, d)
        mod_l, mod_c = mod[:bsz], mod[bsz:]
        g = norm_g[i]
        w_in, w_out = ((ga_w_in, ga_w_out), (hy_w_in, hy_w_out), (ssm_w_in, ssm_w_out), (na_w_in, na_w_out))[kind]
        w_in, w_out = w_in[j].astype(BF16), w_out[j].astype(BF16)
        u_dtype = BF16 if kind == 3 else F32
        u_l = norm_proj(x_lat, mod_l, g, w_in, n_lat, u_dtype).reshape(bsz, n_lat, -1)
        u_c = norm_proj(x_ctx, mod_c, g, w_in, bsz * n_ctx, u_dtype).reshape(bsz, n_ctx, -1)
        if kind == 0:
            a_l = gqa_attention(u_l, u_l, u_c, ga_q_gain[j], ga_k_gain[j])
            a_c = gqa_attention(u_c, None, u_c, ga_q_gain[j], ga_k_gain[j]) if with_ctx else None
        elif kind == 1:
            hy = (hy_conv_w[j], hy_conv_b[j], hy_f_w1[j], hy_f_b1[j], hy_f_w2[j], hy_f_b2[j], hy_f_w3[j],
                  hy_f_freq[j], hy_skip[j])
            a_l = hyena_mixer(u_l, *hy)
            a_c = hyena_mixer(u_c, *hy) if with_ctx else None
        elif kind == 2:
            a_l, a_c = ssd_mixer(u_l, u_c, ssm_conv_w[j], ssm_conv_b[j], ssm_dt_bias[j], ssm_a_log[j],
                                 ssm_d[j], ssm_norm_g[j], with_ctx)
        else:
            a_l = na_attention(u_l, u_c, na_rpb[j])
            a_c = mha_ctx(u_c) if with_ctx else None
        w1, w2 = mlp_w1[i].astype(BF16), mlp_w2[i].astype(BF16)
        x_lat, h_l = out_proj_residual(a_l.reshape(bsz * n_lat, -1), w_out, x_lat, mod_l, g, n_lat)
        x_lat = mlp_residual(h_l, w1, w2, x_lat, mod_l, g, n_lat)
        if with_ctx:
            x_ctx, h_c = out_proj_residual(a_c.reshape(bsz * n_ctx, -1), w_out, x_ctx, mod_c, g, bsz * n_ctx)
            x_ctx = mlp_residual(h_c, w1, w2, x_ctx, mod_c, g, bsz * n_ctx)
    return x_lat.reshape(bsz, n_lat, d)
```

```python
import functools
import math

import jax
import jax.numpy as jnp
from jax import lax
from jax.experimental import pallas as pl
from jax.experimental.pallas import tpu as pltpu

GRID_W = 64
N_MIXERS = 4
RMS_EPS = 1e-6

GA_HEAD_DIM = 128
GA_N_KV = 2
ROPE_THETA = 10000.0

HY_ORDER = 2
HY_BANDS = 16
HY_DECAY_SHORT = 0.3
HY_DECAY_LONG = 1.5
HY_TARGET = 1e-2
HY_TILE = 128
HY_CH_BLOCK = 8

SSM_HEAD_DIM = 64
SSM_GROUPS = 4
SSM_D_STATE = 128
SSM_CHUNK = 128

NA_HEAD_DIM = 64
NA_KH = 8
NA_KW = 16

VMEM_LIMIT_BYTES = 56 * 1024 * 1024
BF16 = jnp.bfloat16
F32 = jnp.float32
HIGHEST = lax.Precision.HIGHEST
NT_DIMS = (((1,), (1,)), ((), ()))
LOG2E = 1.4426950408889634


def _params(n_grid_dims):
    return pltpu.CompilerParams(dimension_semantics=("arbitrary",) * n_grid_dims,
                                vmem_limit_bytes=VMEM_LIMIT_BYTES)


def _row_tile(rows_per_mod, cap):
    t = min(rows_per_mod, cap)
    while rows_per_mod % t:
        t //= 2
    return t


def _col_chunks(n, width):
    return [(c, min(width, n - c)) for c in range(0, n, width)]


def _rms(x, g):
    return x * lax.rsqrt(jnp.mean(x * x, axis=-1, keepdims=True) + RMS_EPS) * g


def _silu(x):
    return x / (1.0 + jnp.exp(-x))


def _dense_kernel(a_ref, w_ref, b_ref, o_ref):
    o_ref[...] = jnp.dot(a_ref[...], w_ref[...], preferred_element_type=F32, precision=HIGHEST) + b_ref[...]


def dense_f32(a, w, b):
    m, k = a.shape
    n = w.shape[1]
    tn = 1024 if n % 1024 == 0 else n
    return pl.pallas_call(
        _dense_kernel,
        grid=(n // tn,),
        in_specs=[pl.BlockSpec((m, k), lambda j: (0, 0)),
                  pl.BlockSpec((k, tn), lambda j: (0, j)),
                  pl.BlockSpec((1, tn), lambda j: (0, j))],
        out_specs=pl.BlockSpec((m, tn), lambda j: (0, j)),
        out_shape=jax.ShapeDtypeStruct((m, n), F32),
        compiler_params=_params(1),
        name="ada_dense",
    )(a, w, b)


def _norm_proj_kernel(x_ref, mod_ref, g_ref, w_ref, o_ref, *, tn):
    x = x_ref[...]
    h = _rms(x, g_ref[0:1, :]) * (1.0 + mod_ref[0, 1:2, :]) + mod_ref[0, 0:1, :]
    h = h.astype(BF16)
    for c, s in _col_chunks(w_ref.shape[1], tn):
        o_ref[:, c:c + s] = jnp.dot(h, w_ref[:, c:c + s], preferred_element_type=F32).astype(o_ref.dtype)


def norm_proj(x, mod, g, w, rows_per_mod, out_dtype=F32):
    m, d = x.shape
    n = w.shape[1]
    tm = _row_tile(rows_per_mod, 512)
    bpm = rows_per_mod // tm
    return pl.pallas_call(
        functools.partial(_norm_proj_kernel, tn=512),
        grid=(m // tm,),
        in_specs=[pl.BlockSpec((tm, d), lambda i: (i, 0)),
                  pl.BlockSpec((1, 6, d), lambda i: (i // bpm, 0, 0)),
                  pl.BlockSpec((4, d), lambda i: (0, 0)),
                  pl.BlockSpec((d, n), lambda i: (0, 0))],
        out_specs=pl.BlockSpec((tm, n), lambda i: (i, 0)),
        out_shape=jax.ShapeDtypeStruct((m, n), out_dtype),
        compiler_params=_params(1),
        name="norm_proj",
    )(x, mod, g, w)


def _out_proj_kernel(a_ref, w_ref, x_ref, mod_ref, g_ref, xo_ref, h_ref):
    y = jnp.dot(a_ref[...].astype(BF16), w_ref[...], preferred_element_type=F32)
    x = x_ref[...] + mod_ref[0, 2:3, :] * _rms(y, g_ref[1:2, :])
    xo_ref[...] = x
    h = _rms(x, g_ref[2:3, :]) * (1.0 + mod_ref[0, 4:5, :]) + mod_ref[0, 3:4, :]
    h_ref[...] = h.astype(BF16)


def out_proj_residual(a, w, x, mod, g, rows_per_mod):
    m, k = a.shape
    d = w.shape[1]
    tm = _row_tile(rows_per_mod, 512)
    bpm = rows_per_mod // tm
    return pl.pallas_call(
        _out_proj_kernel,
        grid=(m // tm,),
        in_specs=[pl.BlockSpec((tm, k), lambda i: (i, 0)),
                  pl.BlockSpec((k, d), lambda i: (0, 0)),
                  pl.BlockSpec((tm, d), lambda i: (i, 0)),
                  pl.BlockSpec((1, 6, d), lambda i: (i // bpm, 0, 0)),
                  pl.BlockSpec((4, d), lambda i: (0, 0))],
        out_specs=[pl.BlockSpec((tm, d), lambda i: (i, 0)),
                   pl.BlockSpec((tm, d), lambda i: (i, 0))],
        out_shape=[jax.ShapeDtypeStruct((m, d), F32), jax.ShapeDtypeStruct((m, d), BF16)],
        compiler_params=_params(1),
        name="out_proj",
    )(a, w, x, mod, g)


def _mlp_kernel(h_ref, w1_ref, w2_ref, x_ref, mod_ref, g_ref, o_ref, *, fc):
    h = h_ref[...]
    acc = jnp.zeros(o_ref.shape, F32)
    for c, s in _col_chunks(w1_ref.shape[1], fc):
        u = jnp.dot(h, w1_ref[:, c:c + s], preferred_element_type=F32)
        u = jnp.square(jnp.maximum(u, 0.0)).astype(BF16)
        acc = acc + jnp.dot(u, w2_ref[c:c + s, :], preferred_element_type=F32)
    o_ref[...] = x_ref[...] + mod_ref[0, 5:6, :] * _rms(acc, g_ref[3:4, :])


def mlp_residual(h, w1, w2, x, mod, g, rows_per_mod):
    m, d = x.shape
    f = w1.shape[1]
    tm = _row_tile(rows_per_mod, 512)
    bpm = rows_per_mod // tm
    return pl.pallas_call(
        functools.partial(_mlp_kernel, fc=1024),
        grid=(m // tm,),
        in_specs=[pl.BlockSpec((tm, d), lambda i: (i, 0)),
                  pl.BlockSpec((d, f), lambda i: (0, 0)),
                  pl.BlockSpec((f, d), lambda i: (0, 0)),
                  pl.BlockSpec((tm, d), lambda i: (i, 0)),
                  pl.BlockSpec((1, 6, d), lambda i: (i // bpm, 0, 0)),
                  pl.BlockSpec((4, d), lambda i: (0, 0))],
        out_specs=pl.BlockSpec((tm, d), lambda i: (i, 0)),
        out_shape=jax.ShapeDtypeStruct((m, d), F32),
        compiler_params=_params(1),
        name="mlp",
    )(h, w1, w2, x, mod, g)


def _rope(x, cos, sin_lo, sin_hi):
    return x * cos + pltpu.roll(x, 96, 1) * sin_lo + pltpu.roll(x, 32, 1) * sin_hi


def _gqa_kernel(*refs, n_lat, group):
    hd = GA_HEAD_DIM
    if n_lat:
        (q_ref, kl_ref, vl_ref, kc_ref, vc_ref, qg_ref, kg_ref, tq_ref, tk_ref, o_ref, k_scr, v_scr) = refs
    else:
        (q_ref, kc_ref, vc_ref, qg_ref, kg_ref, o_ref, k_scr, v_scr) = refs

    @pl.when(pl.program_id(2) == 0)
    def _():
        if n_lat:
            k = _rope(_rms(kl_ref[0], kg_ref[...]), tk_ref[0], tk_ref[1], tk_ref[2])
            k_scr[0:n_lat, :] = k.astype(BF16)
            v_scr[0:n_lat, :] = vl_ref[0].astype(BF16)
        k_scr[n_lat:, :] = _rms(kc_ref[0], kg_ref[...]).astype(BF16)
        v_scr[n_lat:, :] = vc_ref[0].astype(BF16)

    outs = []
    for h in range(group):
        q = _rms(q_ref[0, :, h * hd:(h + 1) * hd], qg_ref[...])
        if n_lat:
            q = _rope(q, tq_ref[0], tq_ref[1], tq_ref[2])
        q = (q * (hd ** -0.5 * LOG2E)).astype(BF16)
        s = lax.dot_general(q, k_scr[...], NT_DIMS, preferred_element_type=F32)
        p = jnp.exp2(s - jnp.max(s, axis=-1, keepdims=True))
        l = jnp.sum(p, axis=-1, keepdims=True)
        o = jnp.dot(p.astype(BF16), v_scr[...], preferred_element_type=F32) / l
        outs.append(o.astype(o_ref.dtype))
    o_ref[0] = jnp.concatenate(outs, axis=1)


def _rope_tables(n_tokens, head_dim):
    t = jnp.arange(n_tokens)
    row = (t // GRID_W).astype(F32)
    col = (t % GRID_W).astype(F32)
    nf = head_dim // 4
    inv = ROPE_THETA ** (-jnp.arange(nf, dtype=F32) / nf)
    ar = row[:, None] * inv[None, :]
    ac = col[:, None] * inv[None, :]
    zero = jnp.zeros_like(ar)
    cos = jnp.concatenate([jnp.cos(ar), jnp.cos(ar), jnp.cos(ac), jnp.cos(ac)], axis=-1)
    sin_lo = jnp.concatenate([-jnp.sin(ar), zero, -jnp.sin(ac), zero], axis=-1)
    sin_hi = jnp.concatenate([zero, jnp.sin(ar), zero, jnp.sin(ac)], axis=-1)
    return jnp.stack([cos, sin_lo, sin_hi])


def gqa_attention(u_q, u_lat, u_ctx, q_gain, k_gain):
    hd = GA_HEAD_DIM
    bsz, nq, width = u_q.shape
    n_heads = width // hd - 2 * GA_N_KV
    group = n_heads // GA_N_KV
    n_lat = 0 if u_lat is None else u_lat.shape[1]
    n_ctx = u_ctx.shape[1]
    tq = _row_tile(nq, 256)
    k_col, v_col = n_heads, n_heads + GA_N_KV
    qg, kg = q_gain.reshape(1, hd), k_gain.reshape(1, hd)
    q_spec = pl.BlockSpec((1, tq, group * hd), lambda b, kv, i: (b, i, kv))
    ctx_specs = [pl.BlockSpec((1, n_ctx, hd), lambda b, kv, i: (b, 0, k_col + kv)),
                 pl.BlockSpec((1, n_ctx, hd), lambda b, kv, i: (b, 0, v_col + kv))]
    gain_specs = [pl.BlockSpec((1, hd), lambda b, kv, i: (0, 0))] * 2
    if n_lat:
        tables = _rope_tables(n_lat, hd)
        in_specs = ([q_spec,
                     pl.BlockSpec((1, n_lat, hd), lambda b, kv, i: (b, 0, k_col + kv)),
                     pl.BlockSpec((1, n_lat, hd), lambda b, kv, i: (b, 0, v_col + kv))]
                    + ctx_specs + gain_specs
                    + [pl.BlockSpec((3, tq, hd), lambda b, kv, i: (0, i, 0)),
                       pl.BlockSpec((3, n_lat, hd), lambda b, kv, i: (0, 0, 0))])
        args = (u_q, u_lat, u_lat, u_ctx, u_ctx, qg, kg, tables, tables)
    else:
        in_specs = [q_spec] + ctx_specs + gain_specs
        args = (u_q, u_ctx, u_ctx, qg, kg)
    return pl.pallas_call(
        functools.partial(_gqa_kernel, n_lat=n_lat, group=group),
        grid=(bsz, GA_N_KV, nq // tq),
        in_specs=in_specs,
        out_specs=pl.BlockSpec((1, tq, group * hd), lambda b, kv, i: (b, i, kv)),
        out_shape=jax.ShapeDtypeStruct((bsz, nq, n_heads * hd), BF16),
        scratch_shapes=[pltpu.VMEM((n_lat + n_ctx, hd), BF16), pltpu.VMEM((n_lat + n_ctx, hd), BF16)],
        compiler_params=_params(3),
        name="gqa_lat" if n_lat else "gqa_ctx",
    )(*args)


def _dwconv_kernel(u_ref, w_ref, b_ref, o_ref, *, silu, transpose_out):
    x = u_ref[0]
    n = x.shape[0]
    row = lax.broadcasted_iota(jnp.int32, x.shape, 0)
    prev = jnp.where(row == 0, 0.0, pltpu.roll(x, 1, 0))
    nxt = jnp.where(row == n - 1, 0.0, pltpu.roll(x, n - 1, 0))
    y = prev * w_ref[0:1, :] + x * w_ref[1:2, :] + nxt * w_ref[2:3, :] + b_ref[...]
    if silu:
        y = _silu(y)
    if transpose_out:
        o_ref[...] = y.T.astype(o_ref.dtype)
    else:
        o_ref[0] = y.astype(o_ref.dtype)


def dwconv3(u, col0, w, b, silu, transpose_out=False):
    bsz, n, _ = u.shape
    ch = w.shape[1]
    tc = 512
    assert ch % tc == 0 and col0 % tc == 0 and w.shape[0] == 3
    if transpose_out:
        out_spec = pl.BlockSpec((tc, n), lambda b_, j: (j, b_))
        out_shape = jax.ShapeDtypeStruct((ch, bsz * n), F32)
    else:
        out_spec = pl.BlockSpec((1, n, tc), lambda b_, j: (b_, 0, j))
        out_shape = jax.ShapeDtypeStruct((bsz, n, ch), F32)
    out = pl.pallas_call(
        functools.partial(_dwconv_kernel, silu=silu, transpose_out=transpose_out),
        grid=(bsz, ch // tc),
        in_specs=[pl.BlockSpec((1, n, tc), lambda b_, j: (b_, 0, col0 // tc + j)),
                  pl.BlockSpec((3, tc), lambda b_, j: (0, j)),
                  pl.BlockSpec((1, tc), lambda b_, j: (0, j))],
        out_specs=out_spec,
        out_shape=out_shape,
        compiler_params=_params(2),
        name="dwconv3",
    )(u, w, b.reshape(1, ch))
    return out.reshape(ch, bsz, n) if transpose_out else out


def _hyena_filter_kernel(feats_ref, w1_ref, b1_ref, w2_ref, b2_ref, fr_ref, w3_ref, dl_ref, o_ref, h_scr):
    @pl.when(pl.program_id(0) == 0)
    def _():
        h = jnp.dot(w1_ref[...], feats_ref[...], precision=HIGHEST, preferred_element_type=F32) + b1_ref[...]
        h = jnp.sin(fr_ref[:, 0:1] * h)
        h = jnp.dot(w2_ref[...], h, precision=HIGHEST, preferred_element_type=F32) + b2_ref[...]
        h_scr[...] = jnp.sin(fr_ref[:, 1:2] * h)

    h = jnp.dot(w3_ref[...], h_scr[...], precision=HIGHEST, preferred_element_type=F32)
    o_ref[...] = h * jnp.exp(-dl_ref[...] * feats_ref[0:1, :])


def hyena_filters(n, d, w1, b1, w2, b2, w3, freq):
    t = jnp.linspace(0.0, 1.0, n, dtype=F32)[None, :]
    w = 2.0 * math.pi * jnp.arange(n, dtype=F32)[None, :] / n
    bands = jnp.linspace(1e-4, HY_BANDS - 1, HY_BANDS, dtype=F32)[:, None]
    feats = jnp.concatenate([t, jnp.cos(bands * w), -jnp.sin(bands * w)], axis=0)
    deltas = jnp.abs(jnp.linspace(math.log(HY_TARGET) / HY_DECAY_SHORT,
                                  math.log(HY_TARGET) / HY_DECAY_LONG, d, dtype=F32))
    n_f = 2 * HY_ORDER * d
    dl = jnp.tile(deltas, 2 * HY_ORDER).reshape(n_f, 1)
    hid = w1.shape[1]
    emb = -(-feats.shape[0] // 8) * 8
    feats = jnp.pad(feats, ((0, emb - feats.shape[0]), (0, 0)))
    w1 = jnp.pad(w1, ((0, emb - w1.shape[0]), (0, 0)))
    tr = 512
    assert n_f % tr == 0
    const = lambda shape: pl.BlockSpec(shape, lambda j: (0, 0))
    return pl.pallas_call(
        _hyena_filter_kernel,
        grid=(n_f // tr,),
        in_specs=[const((emb, n)), const((hid, emb)), const((hid, 1)), const((hid, hid)), const((hid, 1)),
                  const((hid, 2)),
                  pl.BlockSpec((tr, hid), lambda j: (j, 0)),
                  pl.BlockSpec((tr, 1), lambda j: (j, 0))],
        out_specs=pl.BlockSpec((tr, n), lambda j: (j, 0)),
        out_shape=jax.ShapeDtypeStruct((n_f, n), F32),
        scratch_shapes=[pltpu.VMEM((hid, n), F32)],
        compiler_params=_params(1),
        name="hyena_filter",
    )(feats, w1.T, b1.reshape(hid, 1), w2.T, b2.reshape(hid, 1), freq.T, w3.T, dl)


def _toeplitz_conv(z, gline):
    bsz, n = z.shape
    half, t = HY_TILE, 2 * HY_TILE
    m = n // t
    g = pltpu.roll(jnp.broadcast_to(gline, (half, 2 * n)), 0, 1, stride=1, stride_axis=0).astype(BF16)
    zb = z.astype(BF16)
    zr = jnp.concatenate([zb[:, j * t:(j + 1) * t] for j in range(m)], axis=0)
    acc = [None] * m
    for dd in range(-(m - 1), m):
        lo = n + t * dd
        tile = jnp.concatenate([g[:, lo:lo + t], g[:, lo - half:lo - half + t]], axis=0)
        j0, cnt = max(0, -dd), m - abs(dd)
        out = jnp.dot(zr[j0 * bsz:(j0 + cnt) * bsz], tile, preferred_element_type=F32)
        for k in range(cnt):
            i = j0 + dd + k
            piece = out[k * bsz:(k + 1) * bsz]
            acc[i] = piece if acc[i] is None else acc[i] + piece
    return jnp.concatenate(acc, axis=1)


def _hyena_conv_kernel(v_ref, x1_ref, x2_ref, g1_ref, g2_ref, sk_ref, o_ref):
    def body(c, carry):
        v = v_ref[c]
        z = x1_ref[c] * (_toeplitz_conv(v, g1_ref[pl.ds(c, 1), :]) + v * sk_ref[pl.ds(c, 1), 0:1])
        z = x2_ref[c] * (_toeplitz_conv(z, g2_ref[pl.ds(c, 1), :]) + z * sk_ref[pl.ds(c, 1), 1:2])
        o_ref[c] = z.astype(o_ref.dtype)
        return carry

    lax.fori_loop(0, v_ref.shape[0], body, 0)


def hyena_mixer(u, conv_w, conv_b, f_w1, f_b1, f_w2, f_b2, f_w3, f_freq, skip):
    bsz, n, d3 = u.shape
    d = d3 // (HY_ORDER + 1)
    dc = HY_CH_BLOCK
    assert d % dc == 0 and n % (2 * HY_TILE) == 0
    uc = dwconv3(u, 0, conv_w, conv_b, silu=False, transpose_out=True)
    filt = hyena_filters(n, d, f_w1, f_b1, f_w2, f_b2, f_w3, f_freq).reshape(2 * HY_ORDER, d, n)

    def gline(fwd, bwd):
        return jnp.concatenate([jnp.zeros((d, 1), F32), jnp.flip(bwd[:, 1:], axis=1), fwd], axis=1)

    g1, g2 = gline(filt[0], filt[1]), gline(filt[2], filt[3])
    nb = d // dc
    z = pl.pallas_call(
        _hyena_conv_kernel,
        grid=(nb,),
        in_specs=[pl.BlockSpec((dc, bsz, n), lambda j: (j, 0, 0)),
                  pl.BlockSpec((dc, bsz, n), lambda j: (nb + j, 0, 0)),
                  pl.BlockSpec((dc, bsz, n), lambda j: (2 * nb + j, 0, 0)),
                  pl.BlockSpec((dc, 2 * n), lambda j: (j, 0)),
                  pl.BlockSpec((dc, 2 * n), lambda j: (j, 0)),
                  pl.BlockSpec((dc, HY_ORDER), lambda j: (j, 0))],
        out_specs=pl.BlockSpec((dc, bsz, n), lambda j: (j, 0, 0)),
        out_shape=jax.ShapeDtypeStruct((d, bsz, n), BF16),
        compiler_params=_params(1),
        name="hyena_conv",
    )(uc, uc, uc, g1, g2, skip.T)
    return z.transpose(1, 2, 0)


def _softplus(x):
    return jnp.maximum(x, 0.0) + jnp.log(1.0 + jnp.exp(-jnp.abs(x)))


def _ssd_kernel(*refs, reverse, epilogue, heads_per_group):
    hp_dim, ns, q = SSM_HEAD_DIM, SSM_D_STATE, SSM_CHUNK
    if epilogue:
        (xs_ref, bc_ref, dt_ref, dtt_ref, dtb_ref, dtbt_ref, al_ref, alt_ref, s0_ref,
         yp_ref, z_ref, dsk_ref, ng_ref, y_ref, sf_ref, s_scr) = refs
    else:
        (xs_ref, bc_ref, dt_ref, dtt_ref, dtb_ref, dtbt_ref, al_ref, alt_ref, s0_ref,
         y_ref, sf_ref, s_scr) = refs
    c = pl.program_id(1)
    n_groups = s_scr.shape[0]
    n_heads = n_groups * heads_per_group
    gw = heads_per_group * hp_dim

    @pl.when(c == 0)
    def _():
        s_scr[...] = s0_ref[0]

    li = lax.broadcasted_iota(jnp.int32, (q, q), 0)
    si = lax.broadcasted_iota(jnp.int32, (q, q), 1)
    keep = (si >= li) if reverse else (si <= li)
    tri = keep.astype(F32)
    tri_t = ((li >= si) if reverse else (li <= si)).astype(F32)
    edge = 0 if reverse else q - 1

    dt = _softplus(dt_ref[0] + dtb_ref[...])
    da = dt * -jnp.exp(al_ref[...])
    a_cs = jnp.dot(tri, da, precision=HIGHEST, preferred_element_type=F32)
    da_t = _softplus(dtt_ref[0] + dtbt_ref[...]) * -jnp.exp(alt_ref[...])
    a_cs_t = jnp.dot(da_t, tri_t, precision=HIGHEST, preferred_element_type=F32)

    eh = lax.broadcasted_iota(jnp.int32, (n_heads, n_heads * hp_dim), 0)
    el = lax.broadcasted_iota(jnp.int32, (n_heads, n_heads * hp_dim), 1)
    expand_p = (el // hp_dim == eh).astype(BF16)
    eh2 = lax.broadcasted_iota(jnp.int32, (n_heads, n_heads * q), 0)
    el2 = lax.broadcasted_iota(jnp.int32, (n_heads, n_heads * q), 1)
    expand_q = (el2 // q == eh2).astype(BF16)

    def spread(v, e):
        return jnp.dot(v.astype(BF16), e, preferred_element_type=F32)

    a_hi = a_cs.astype(BF16)
    a_r1 = a_cs - a_hi.astype(F32)
    a_mid = a_r1.astype(BF16)
    a_lo = a_r1 - a_mid.astype(F32)
    acs_tile = spread(a_hi, expand_q) + spread(a_mid, expand_q) + spread(a_lo, expand_q)
    tot = a_cs[edge:edge + 1, :]
    dt_full = spread(dt, expand_p)
    st_full = spread(jnp.exp(tot - a_cs), expand_p)
    exp_a = spread(jnp.exp(a_cs), expand_p)
    tot_full = jnp.dot(jnp.broadcast_to(tot, (8, n_heads)), expand_p.astype(F32), precision=HIGHEST,
                       preferred_element_type=F32)[0:1, :]

    xs = xs_ref[0]
    x_dt = xs * dt_full
    x_in = x_dt.astype(BF16)
    x_st = (x_dt * st_full).astype(BF16)
    lane = lax.broadcasted_iota(jnp.int32, (q, 2 * hp_dim), 1)
    first = lane < hp_dim
    gn = n_groups * ns
    ys = []
    for g in range(n_groups):
        b_g = bc_ref[0, :, g * ns:(g + 1) * ns]
        c_g = bc_ref[0, :, gn + g * ns:gn + (g + 1) * ns].astype(BF16)
        cb = lax.dot_general(c_g, b_g.astype(BF16), NT_DIMS, preferred_element_type=F32)
        s_prev = s_scr[g]
        y_off = jnp.dot(c_g, s_prev.astype(BF16), preferred_element_type=F32)
        gl = slice(g * gw, (g + 1) * gw)
        s_new = jnp.dot(b_g.T.astype(BF16), x_st[:, gl], preferred_element_type=F32)
        s_scr[g] = jnp.exp(tot_full[:, gl]) * s_prev + s_new
        for e in range(0, heads_per_group, 2):
            ms = []
            for h in (g * heads_per_group + e, g * heads_per_group + e + 1):
                diff = acs_tile[:, h * q:(h + 1) * q] - a_cs_t[h:h + 1, :]
                lmat = jnp.where(keep, jnp.exp(jnp.minimum(diff, 0.0)), 0.0)
                ms.append((cb * lmat).astype(BF16))
            pl_ = slice(g * gw + e * hp_dim, g * gw + (e + 2) * hp_dim)
            yd = jnp.dot(jnp.concatenate(ms, axis=0), x_in[:, pl_], preferred_element_type=F32)
            yd = jnp.where(first, yd[:q], yd[q:])
            y = yd + y_off[:, e * hp_dim:(e + 2) * hp_dim] * exp_a[:, pl_]
            if epilogue:
                y = y + yp_ref[0, :, pl_] + xs[:, pl_] * dsk_ref[:, pl_]
                y = y * _silu(z_ref[0, :, pl_])
            ys.append(y)

    y_all = jnp.concatenate(ys, axis=1)
    y_ref[0] = _rms(y_all, ng_ref[...]).astype(y_ref.dtype) if epilogue else y_all

    @pl.when(c == pl.num_programs(1) - 1)
    def _():
        sf_ref[0] = s_scr[...]


def ssd_scan_dir(xbc, dt, dt_bias, a_log, s0, reverse, epi=None):
    bsz, n, _ = xbc.shape
    n_heads = dt.shape[-1]
    n_groups = SSM_GROUPS
    hpg = n_heads // n_groups
    d_inner = n_heads * SSM_HEAD_DIM
    gn = n_groups * SSM_D_STATE
    q = SSM_CHUNK
    nc = n // q
    assert d_inner % (2 * gn) == 0 and hpg % 2 == 0 and n % q == 0
    cix = (lambda c: nc - 1 - c) if reverse else (lambda c: c)
    dt_t = dt.swapaxes(1, 2)
    in_specs = [pl.BlockSpec((1, q, d_inner), lambda b, c: (b, cix(c), 0)),
                pl.BlockSpec((1, q, 2 * gn), lambda b, c: (b, cix(c), d_inner // (2 * gn))),
                pl.BlockSpec((1, q, n_heads), lambda b, c: (b, cix(c), 0)),
                pl.BlockSpec((1, n_heads, q), lambda b, c: (b, 0, cix(c))),
                pl.BlockSpec((1, n_heads), lambda b, c: (0, 0)),
                pl.BlockSpec((n_heads, 1), lambda b, c: (0, 0)),
                pl.BlockSpec((1, n_heads), lambda b, c: (0, 0)),
                pl.BlockSpec((n_heads, 1), lambda b, c: (0, 0)),
                pl.BlockSpec((1,) + s0.shape[1:], lambda b, c: (b, 0, 0, 0))]
    args = [xbc, xbc, dt, dt_t, dt_bias.reshape(1, n_heads), dt_bias.reshape(n_heads, 1),
            a_log.reshape(1, n_heads), a_log.reshape(n_heads, 1), s0]
    y_dtype = F32
    if epi is not None:
        y_prev, u, d_skip, norm_g = epi
        in_specs += [pl.BlockSpec((1, q, d_inner), lambda b, c: (b, cix(c), 0)),
                     pl.BlockSpec((1, q, d_inner), lambda b, c: (b, cix(c), 0)),
                     pl.BlockSpec((1, d_inner), lambda b, c: (0, 0)),
                     pl.BlockSpec((1, d_inner), lambda b, c: (0, 0))]
        args += [y_prev, u, jnp.repeat(d_skip, SSM_HEAD_DIM).reshape(1, d_inner), norm_g.reshape(1, d_inner)]
        y_dtype = BF16
    return pl.pallas_call(
        functools.partial(_ssd_kernel, reverse=reverse, epilogue=epi is not None, heads_per_group=hpg),
        grid=(bsz, nc),
        in_specs=in_specs,
        out_specs=[pl.BlockSpec((1, q, d_inner), lambda b, c: (b, cix(c), 0)),
                   pl.BlockSpec((1,) + s0.shape[1:], lambda b, c: (b, 0, 0, 0))],
        out_shape=[jax.ShapeDtypeStruct((bsz, n, d_inner), y_dtype), jax.ShapeDtypeStruct(s0.shape, F32)],
        scratch_shapes=[pltpu.VMEM(s0.shape[1:], F32)],
        compiler_params=_params(2),
        name="ssd_bwd" if reverse else "ssd_fwd",
    )(*args)


def ssd_mixer(u_lat, u_ctx, conv_w, conv_b, dt_bias, a_log, d_skip, norm_g, with_ctx_out):
    bsz = u_lat.shape[0]
    n_heads = a_log.shape[-1]
    d_inner = n_heads * SSM_HEAD_DIM
    gn = SSM_GROUPS * SSM_D_STATE
    dt0 = 2 * d_inner + 2 * gn
    xbc_c = dwconv3(u_ctx, d_inner, conv_w, conv_b, silu=True)
    xbc_l = dwconv3(u_lat, d_inner, conv_w, conv_b, silu=True)
    zero = jnp.zeros((bsz, SSM_GROUPS, SSM_D_STATE, d_inner // SSM_GROUPS), F32)

    def dts(u, d):
        return u[:, :, dt0 + d * n_heads:dt0 + (d + 1) * n_heads]

    epi_args = (d_skip, norm_g)
    y_cf, s_f = ssd_scan_dir(xbc_c, dts(u_ctx, 0), dt_bias[0], a_log[0], zero, False)
    y_c, s_b = ssd_scan_dir(xbc_c, dts(u_ctx, 1), dt_bias[1], a_log[1], zero, True, (y_cf, u_ctx) + epi_args)
    y_lf, _ = ssd_scan_dir(xbc_l, dts(u_lat, 0), dt_bias[0], a_log[0], s_f, False)
    y_l, _ = ssd_scan_dir(xbc_l, dts(u_lat, 1), dt_bias[1], a_log[1], s_b, True, (y_lf, u_lat) + epi_args)
    return y_l, (y_c if with_ctx_out else None)


NEG_BIG = -1e30


def _split_heads(q, first):
    zero = jnp.zeros_like(q)
    return jnp.concatenate([jnp.where(first, q, zero), jnp.where(first, zero, q)], axis=0)


def _na_kernel(q_ref, k_ref, v_ref, kc_ref, vc_ref, bias_ref, o_ref, *, rows, n_pairs):
    r = pl.program_id(1)
    r0 = jnp.clip(r - NA_KH // 2, 0, rows - NA_KH)
    start = pl.multiple_of(r0 * GRID_W, GRID_W)
    band = NA_KH * GRID_W
    scale = NA_HEAD_DIM ** -0.5 * LOG2E
    lane = lax.broadcasted_iota(jnp.int32, (GRID_W, 2 * NA_HEAD_DIM), 1)
    first = lane < NA_HEAD_DIM
    outs = []
    for hp in range(n_pairs):
        cols = slice(hp * 2 * NA_HEAD_DIM, (hp + 1) * 2 * NA_HEAD_DIM)
        q2 = _split_heads((q_ref[0, :, cols].astype(F32) * scale).astype(BF16), first)
        s_lat = lax.dot_general(q2, k_ref[0, pl.ds(start, band), cols], NT_DIMS, preferred_element_type=F32)
        s_lat = s_lat + bias_ref[0, hp]
        s_ctx = lax.dot_general(q2, kc_ref[0, :, cols], NT_DIMS, preferred_element_type=F32)
        m = jnp.maximum(jnp.max(s_lat, axis=-1, keepdims=True), jnp.max(s_ctx, axis=-1, keepdims=True))
        p_lat = jnp.exp2(s_lat - m)
        p_ctx = jnp.exp2(s_ctx - m)
        l = jnp.sum(p_lat, axis=-1, keepdims=True) + jnp.sum(p_ctx, axis=-1, keepdims=True)
        o = (jnp.dot(p_lat.astype(BF16), v_ref[0, pl.ds(start, band), cols], preferred_element_type=F32)
             + jnp.dot(p_ctx.astype(BF16), vc_ref[0, :, cols], preferred_element_type=F32)) / l
        outs.append(jnp.where(first, o[:GRID_W], o[GRID_W:]).astype(o_ref.dtype))
    o_ref[0] = jnp.concatenate(outs, axis=1)


def _na_bias(rpb):
    n_heads = rpb.shape[0]
    col = jnp.arange(GRID_W)
    c0 = jnp.clip(col - NA_KW // 2, 0, GRID_W - NA_KW)
    col_ok = (col[None, :] >= c0[:, None]) & (col[None, :] < c0[:, None] + NA_KW)
    dx_idx = jnp.clip(col[None, :] - col[:, None] + NA_KW - 1, 0, 2 * NA_KW - 2)
    rpb_cols = rpb.astype(F32)[:, :, dx_idx]
    dy_idx = jnp.arange(NA_KH)[None, :] - jnp.arange(NA_KH)[:, None] + NA_KH - 1
    bias = rpb_cols[:, dy_idx]
    bias = jnp.where(col_ok[None, None, None], bias * LOG2E, NEG_BIG)
    return bias.transpose(1, 0, 3, 2, 4).reshape(NA_KH, n_heads // 2, 2 * GRID_W, NA_KH * GRID_W)


def na_attention(u_lat, u_ctx, rpb):
    bsz, n_lat, d3 = u_lat.shape
    d = d3 // 3
    n_ctx = u_ctx.shape[1]
    rows = n_lat // GRID_W
    assert rows >= NA_KH and d % (2 * NA_HEAD_DIM) == 0
    n_pairs = d // (2 * NA_HEAD_DIM)
    bias = _na_bias(rpb)

    def variant(r):
        return r - jnp.clip(r - NA_KH // 2, 0, rows - NA_KH)

    return pl.pallas_call(
        functools.partial(_na_kernel, rows=rows, n_pairs=n_pairs),
        grid=(bsz, rows),
        in_specs=[pl.BlockSpec((1, GRID_W, d), lambda b, r: (b, r, 0)),
                  pl.BlockSpec((1, n_lat, d), lambda b, r: (b, 0, 1)),
                  pl.BlockSpec((1, n_lat, d), lambda b, r: (b, 0, 2)),
                  pl.BlockSpec((1, n_ctx, d), lambda b, r: (b, 0, 1)),
                  pl.BlockSpec((1, n_ctx, d), lambda b, r: (b, 0, 2)),
                  pl.BlockSpec((1, n_pairs, 2 * GRID_W, NA_KH * GRID_W), lambda b, r: (variant(r), 0, 0, 0))],
        out_specs=pl.BlockSpec((1, GRID_W, d), lambda b, r: (b, r, 0)),
        out_shape=jax.ShapeDtypeStruct((bsz, n_lat, d), BF16),
        compiler_params=_params(2),
        name="na_attention",
    )(u_lat, u_lat, u_lat, u_ctx, u_ctx, bias)


def _mha_ctx_kernel(q_ref, k_ref, v_ref, o_ref, *, n_pairs):
    n = q_ref.shape[1]
    scale = NA_HEAD_DIM ** -0.5
    lane = lax.broadcasted_iota(jnp.int32, (n, 2 * NA_HEAD_DIM), 1)
    first = lane < NA_HEAD_DIM
    for hp in range(n_pairs):
        cols = slice(hp * 2 * NA_HEAD_DIM, (hp + 1) * 2 * NA_HEAD_DIM)
        q2 = _split_heads(q_ref[0, :, cols], first)
        s = lax.dot_general(q2, k_ref[0, :, cols], NT_DIMS, preferred_element_type=F32) * scale
        p = jnp.exp(s - jnp.max(s, axis=-1, keepdims=True))
        l = jnp.sum(p, axis=-1, keepdims=True)
        o = jnp.dot(p.astype(BF16), v_ref[0, :, cols], preferred_element_type=F32) / l
        o_ref[0, :, cols] = jnp.where(first, o[:n], o[n:]).astype(o_ref.dtype)


def mha_ctx(u_ctx):
    bsz, n, d3 = u_ctx.shape
    d = d3 // 3
    n_pairs = d // (2 * NA_HEAD_DIM)
    return pl.pallas_call(
        functools.partial(_mha_ctx_kernel, n_pairs=n_pairs),
        grid=(bsz,),
        in_specs=[pl.BlockSpec((1, n, d), lambda b: (b, 0, 0)),
                  pl.BlockSpec((1, n, d), lambda b: (b, 0, 1)),
                  pl.BlockSpec((1, n, d), lambda b: (b, 0, 2))],
        out_specs=pl.BlockSpec((1, n, d), lambda b: (b, 0, 0)),
        out_shape=jax.ShapeDtypeStruct((bsz, n, d), BF16),
        compiler_params=_params(1),
        name="mha_ctx",
    )(u_ctx, u_ctx, u_ctx)


def kernel(x, c, ctx, c_ctx, ada_w, ada_b, norm_g, mlp_w1, mlp_w2, ga_w_in, ga_q_gain, ga_k_gain, ga_w_out, hy_w_in, hy_conv_w, hy_conv_b, hy_f_w1, hy_f_b1, hy_f_w2, hy_f_b2, hy_f_w3, hy_f_freq, hy_skip, hy_w_out, ssm_w_in, ssm_conv_w, ssm_conv_b, ssm_dt_bias, ssm_a_log, ssm_d, ssm_norm_g, ssm_w_out, na_w_in, na_rpb, na_w_out):
    bsz, n_lat, d = x.shape
    n_ctx = ctx.shape[1]
    depth = ada_w.shape[0]
    sc = jnp.concatenate([jax.nn.silu(c), jax.nn.silu(c_ctx)[None, :]], axis=0)
    x_lat = x.reshape(bsz * n_lat, d)
    x_ctx = ctx.reshape(bsz * n_ctx, d)
    for i in range(depth):
        kind, j = i % N_MIXERS, i // N_MIXERS
        with_ctx = i < depth - 1
        mod = dense_f32(sc, ada_w[i], ada_b[i][None, :]).reshape(bsz + 1, 6, d)
        mod_l, mod_c = mod[:bsz], mod[bsz:]
        g = norm_g[i]
        w_in, w_out = ((ga_w_in, ga_w_out), (hy_w_in, hy_w_out), (ssm_w_in, ssm_w_out), (na_w_in, na_w_out))[kind]
        w_in, w_out = w_in[j].astype(BF16), w_out[j].astype(BF16)
        u_dtype = BF16 if kind == 3 else F32
        u_l = norm_proj(x_lat, mod_l, g, w_in, n_lat, u_dtype).reshape(bsz, n_lat, -1)
        u_c = norm_proj(x_ctx, mod_c, g, w_in, bsz * n_ctx, u_dtype).reshape(bsz, n_ctx, -1)
        if kind == 0:
            a_l = gqa_attention(u_l, u_l, u_c, ga_q_gain[j], ga_k_gain[j])
            a_c = gqa_attention(u_c, None, u_c, ga_q_gain[j], ga_k_gain[j]) if with_ctx else None
        elif kind == 1:
            hy = (hy_conv_w[j], hy_conv_b[j], hy_f_w1[j], hy_f_b1[j], hy_f_w2[j], hy_f_b2[j], hy_f_w3[j],
                  hy_f_freq[j], hy_skip[j])
            a_l = hyena_mixer(u_l, *hy)
            a_c = hyena_mixer(u_c, *hy) if with_ctx else None
        elif kind == 2:
            a_l, a_c = ssd_mixer(u_l, u_c, ssm_conv_w[j], ssm_conv_b[j], ssm_dt_bias[j], ssm_a_log[j],
                                 ssm_d[j], ssm_norm_g[j], with_ctx)
        else:
            a_l = na_attention(u_l, u_c, na_rpb[j])
            a_c = mha_ctx(u_c) if with_ctx else None
        w1, w2 = mlp_w1[i].astype(BF16), mlp_w2[i].astype(BF16)
        x_lat, h_l = out_proj_residual(a_l.reshape(bsz * n_lat, -1), w_out, x_lat, mod_l, g, n_lat)
        x_lat = mlp_residual(h_l, w1, w2, x_lat, mod_l, g, n_lat)
        if with_ctx:
            x_ctx, h_c = out_proj_residual(a_c.reshape(bsz * n_ctx, -1), w_out, x_ctx, mod_c, g, bsz * n_ctx)
            x_ctx = mlp_residual(h_c, w1, w2, x_ctx, mod_c, g, bsz * n_ctx)
    return x_lat.reshape(bsz, n_lat, d)
```

```python
import functools
import math

import jax
import jax.numpy as jnp
from jax import lax
from jax.experimental import pallas as pl
from jax.experimental.pallas import tpu as pltpu

GRID_W = 64
N_MIXERS = 4
RMS_EPS = 1e-6

GA_HEAD_DIM = 128
GA_N_KV = 2
ROPE_THETA = 10000.0

HY_ORDER = 2
HY_BANDS = 16
HY_DECAY_SHORT = 0.3
HY_DECAY_LONG = 1.5
HY_TARGET = 1e-2
HY_TILE = 128
HY_CH_BLOCK = 8

SSM_HEAD_DIM = 64
SSM_GROUPS = 4
SSM_D_STATE = 128
SSM_CHUNK = 128

NA_HEAD_DIM = 64
NA_KH = 8
NA_KW = 16

VMEM_LIMIT_BYTES = 56 * 1024 * 1024
BF16 = jnp.bfloat16
F32 = jnp.float32
HIGHEST = lax.Precision.HIGHEST
NT_DIMS = (((1,), (1,)), ((), ()))
LOG2E = 1.4426950408889634


def _params(n_grid_dims):
    return pltpu.CompilerParams(dimension_semantics=("arbitrary",) * n_grid_dims,
                                vmem_limit_bytes=VMEM_LIMIT_BYTES)


def _row_tile(rows_per_mod, cap):
    t = min(rows_per_mod, cap)
    while rows_per_mod % t:
        t //= 2
    return t


def _col_chunks(n, width):
    return [(c, min(width, n - c)) for c in range(0, n, width)]


def _rms(x, g):
    return x * lax.rsqrt(jnp.mean(x * x, axis=-1, keepdims=True) + RMS_EPS) * g


def _silu(x):
    return x / (1.0 + jnp.exp(-x))


def _dense_kernel(a_ref, w_ref, b_ref, o_ref):
    o_ref[...] = jnp.dot(a_ref[...], w_ref[...], preferred_element_type=F32, precision=HIGHEST) + b_ref[...]


def dense_f32(a, w, b):
    m, k = a.shape
    n = w.shape[1]
    tn = 1024 if n % 1024 == 0 else n
    return pl.pallas_call(
        _dense_kernel,
        grid=(n // tn,),
        in_specs=[pl.BlockSpec((m, k), lambda j: (0, 0)),
                  pl.BlockSpec((k, tn), lambda j: (0, j)),
                  pl.BlockSpec((1, tn), lambda j: (0, j))],
        out_specs=pl.BlockSpec((m, tn), lambda j: (0, j)),
        out_shape=jax.ShapeDtypeStruct((m, n), F32),
        compiler_params=_params(1),
        name="ada_dense",
    )(a, w, b)


def _norm_proj_kernel(x_ref, mod_ref, g_ref, w_ref, o_ref, *, tn):
    x = x_ref[...]
    h = _rms(x, g_ref[0:1, :]) * (1.0 + mod_ref[0, 1:2, :]) + mod_ref[0, 0:1, :]
    h = h.astype(BF16)
    for c, s in _col_chunks(w_ref.shape[1], tn):
        o_ref[:, c:c + s] = jnp.dot(h, w_ref[:, c:c + s], preferred_element_type=F32).astype(o_ref.dtype)


def norm_proj(x, mod, g, w, rows_per_mod, out_dtype=F32):
    m, d = x.shape
    n = w.shape[1]
    tm = _row_tile(rows_per_mod, 512)
    bpm = rows_per_mod // tm
    return pl.pallas_call(
        functools.partial(_norm_proj_kernel, tn=512),
        grid=(m // tm,),
        in_specs=[pl.BlockSpec((tm, d), lambda i: (i, 0)),
                  pl.BlockSpec((1, 6, d), lambda i: (i // bpm, 0, 0)),
                  pl.BlockSpec((4, d), lambda i: (0, 0)),
                  pl.BlockSpec((d, n), lambda i: (0, 0))],
        out_specs=pl.BlockSpec((tm, n), lambda i: (i, 0)),
        out_shape=jax.ShapeDtypeStruct((m, n), out_dtype),
        compiler_params=_params(1),
        name="norm_proj",
    )(x, mod, g, w)


def _out_proj_kernel(a_ref, w_ref, x_ref, mod_ref, g_ref, xo_ref, h_ref):
    y = jnp.dot(a_ref[...].astype(BF16), w_ref[...], preferred_element_type=F32)
    x = x_ref[...] + mod_ref[0, 2:3, :] * _rms(y, g_ref[1:2, :])
    xo_ref[...] = x
    h = _rms(x, g_ref[2:3, :]) * (1.0 + mod_ref[0, 4:5, :]) + mod_ref[0, 3:4, :]
    h_ref[...] = h.astype(BF16)


def out_proj_residual(a, w, x, mod, g, rows_per_mod):
    m, k = a.shape
    d = w.shape[1]
    tm = _row_tile(rows_per_mod, 512)
    bpm = rows_per_mod // tm
    return pl.pallas_call(
        _out_proj_kernel,
        grid=(m // tm,),
        in_specs=[pl.BlockSpec((tm, k), lambda i: (i, 0)),
                  pl.BlockSpec((k, d), lambda i: (0, 0)),
                  pl.BlockSpec((tm, d), lambda i: (i, 0)),
                  pl.BlockSpec((1, 6, d), lambda i: (i // bpm, 0, 0)),
                  pl.BlockSpec((4, d), lambda i: (0, 0))],
        out_specs=[pl.BlockSpec((tm, d), lambda i: (i, 0)),
                   pl.BlockSpec((tm, d), lambda i: (i, 0))],
        out_shape=[jax.ShapeDtypeStruct((m, d), F32), jax.ShapeDtypeStruct((m, d), BF16)],
        compiler_params=_params(1),
        name="out_proj",
    )(a, w, x, mod, g)


def _mlp_kernel(h_ref, w1_ref, w2_ref, x_ref, mod_ref, g_ref, o_ref, *, fc):
    h = h_ref[...]
    acc = jnp.zeros(o_ref.shape, F32)
    for c, s in _col_chunks(w1_ref.shape[1], fc):
        u = jnp.dot(h, w1_ref[:, c:c + s], preferred_element_type=F32)
        u = jnp.square(jnp.maximum(u, 0.0)).astype(BF16)
        acc = acc + jnp.dot(u, w2_ref[c:c + s, :], preferred_element_type=F32)
    o_ref[...] = x_ref[...] + mod_ref[0, 5:6, :] * _rms(acc, g_ref[3:4, :])


def mlp_residual(h, w1, w2, x, mod, g, rows_per_mod):
    m, d = x.shape
    f = w1.shape[1]
    tm = _row_tile(rows_per_mod, 512)
    bpm = rows_per_mod // tm
    return pl.pallas_call(
        functools.partial(_mlp_kernel, fc=1024),
        grid=(m // tm,),
        in_specs=[pl.BlockSpec((tm, d), lambda i: (i, 0)),
                  pl.BlockSpec((d, f), lambda i: (0, 0)),
                  pl.BlockSpec((f, d), lambda i: (0, 0)),
                  pl.BlockSpec((tm, d), lambda i: (i, 0)),
                  pl.BlockSpec((1, 6, d), lambda i: (i // bpm, 0, 0)),
                  pl.BlockSpec((4, d), lambda i: (0, 0))],
        out_specs=pl.BlockSpec((tm, d), lambda i: (i, 0)),
        out_shape=jax.ShapeDtypeStruct((m, d), F32),
        compiler_params=_params(1),
        name="mlp",
    )(h, w1, w2, x, mod, g)


def _rope(x, cos, sin_lo, sin_hi):
    return x * cos + pltpu.roll(x, 96, 1) * sin_lo + pltpu.roll(x, 32, 1) * sin_hi


def _gqa_kernel(*refs, n_lat, group):
    hd = GA_HEAD_DIM
    if n_lat:
        (q_ref, kl_ref, vl_ref, kc_ref, vc_ref, qg_ref, kg_ref, tq_ref, tk_ref, o_ref, k_scr, v_scr) = refs
    else:
        (q_ref, kc_ref, vc_ref, qg_ref, kg_ref, o_ref, k_scr, v_scr) = refs

    @pl.when(pl.program_id(2) == 0)
    def _():
        if n_lat:
            k = _rope(_rms(kl_ref[0], kg_ref[...]), tk_ref[0], tk_ref[1], tk_ref[2])
            k_scr[0:n_lat, :] = k.astype(BF16)
            v_scr[0:n_lat, :] = vl_ref[0].astype(BF16)
        k_scr[n_lat:, :] = _rms(kc_ref[0], kg_ref[...]).astype(BF16)
        v_scr[n_lat:, :] = vc_ref[0].astype(BF16)

    outs = []
    for h in range(group):
        q = _rms(q_ref[0, :, h * hd:(h + 1) * hd], qg_ref[...])
        if n_lat:
            q = _rope(q, tq_ref[0], tq_ref[1], tq_ref[2])
        q = (q * (hd ** -0.5 * LOG2E)).astype(BF16)
        s = lax.dot_general(q, k_scr[...], NT_DIMS, preferred_element_type=F32)
        p = jnp.exp2(s - jnp.max(s, axis=-1, keepdims=True))
        l = jnp.sum(p, axis=-1, keepdims=True)
        o = jnp.dot(p.astype(BF16), v_scr[...], preferred_element_type=F32) / l
        outs.append(o.astype(o_ref.dtype))
    o_ref[0] = jnp.concatenate(outs, axis=1)


def _rope_tables(n_tokens, head_dim):
    t = jnp.arange(n_tokens)
    row = (t // GRID_W).astype(F32)
    col = (t % GRID_W).astype(F32)
    nf = head_dim // 4
    inv = ROPE_THETA ** (-jnp.arange(nf, dtype=F32) / nf)
    ar = row[:, None] * inv[None, :]
    ac = col[:, None] * inv[None, :]
    zero = jnp.zeros_like(ar)
    cos = jnp.concatenate([jnp.cos(ar), jnp.cos(ar), jnp.cos(ac), jnp.cos(ac)], axis=-1)
    sin_lo = jnp.concatenate([-jnp.sin(ar), zero, -jnp.sin(ac), zero], axis=-1)
    sin_hi = jnp.concatenate([zero, jnp.sin(ar), zero, jnp.sin(ac)], axis=-1)
    return jnp.stack([cos, sin_lo, sin_hi])


def gqa_attention(u_q, u_lat, u_ctx, q_gain, k_gain):
    hd = GA_HEAD_DIM
    bsz, nq, width = u_q.shape
    n_heads = width // hd - 2 * GA_N_KV
    group = n_heads // GA_N_KV
    n_lat = 0 if u_lat is None else u_lat.shape[1]
    n_ctx = u_ctx.shape[1]
    tq = _row_tile(nq, 256)
    k_col, v_col = n_heads, n_heads + GA_N_KV
    qg, kg = q_gain.reshape(1, hd), k_gain.reshape(1, hd)
    q_spec = pl.BlockSpec((1, tq, group * hd), lambda b, kv, i: (b, i, kv))
    ctx_specs = [pl.BlockSpec((1, n_ctx, hd), lambda b, kv, i: (b, 0, k_col + kv)),
                 pl.BlockSpec((1, n_ctx, hd), lambda b, kv, i: (b, 0, v_col + kv))]
    gain_specs = [pl.BlockSpec((1, hd), lambda b, kv, i: (0, 0))] * 2
    if n_lat:
        tables = _rope_tables(n_lat, hd)
        in_specs = ([q_spec,
                     pl.BlockSpec((1, n_lat, hd), lambda b, kv, i: (b, 0, k_col + kv)),
                     pl.BlockSpec((1, n_lat, hd), lambda b, kv, i: (b, 0, v_col + kv))]
                    + ctx_specs + gain_specs
                    + [pl.BlockSpec((3, tq, hd), lambda b, kv, i: (0, i, 0)),
                       pl.BlockSpec((3, n_lat, hd), lambda b, kv, i: (0, 0, 0))])
        args = (u_q, u_lat, u_lat, u_ctx, u_ctx, qg, kg, tables, tables)
    else:
        in_specs = [q_spec] + ctx_specs + gain_specs
        args = (u_q, u_ctx, u_ctx, qg, kg)
    return pl.pallas_call(
        functools.partial(_gqa_kernel, n_lat=n_lat, group=group),
        grid=(bsz, GA_N_KV, nq // tq),
        in_specs=in_specs,
        out_specs=pl.BlockSpec((1, tq, group * hd), lambda b, kv, i: (b, i, kv)),
        out_shape=jax.ShapeDtypeStruct((bsz, nq, n_heads * hd), BF16),
        scratch_shapes=[pltpu.VMEM((n_lat + n_ctx, hd), BF16), pltpu.VMEM((n_lat + n_ctx, hd), BF16)],
        compiler_params=_params(3),
        name="gqa_lat" if n_lat else "gqa_ctx",
    )(*args)


def _dwconv_kernel(u_ref, w_ref, b_ref, o_ref, *, silu, transpose_out):
    x = u_ref[0]
    n = x.shape[0]
    row = lax.broadcasted_iota(jnp.int32, x.shape, 0)
    prev = jnp.where(row == 0, 0.0, pltpu.roll(x, 1, 0))
    nxt = jnp.where(row == n - 1, 0.0, pltpu.roll(x, n - 1, 0))
    y = prev * w_ref[0:1, :] + x * w_ref[1:2, :] + nxt * w_ref[2:3, :] + b_ref[...]
    if silu:
        y = _silu(y)
    if transpose_out:
        o_ref[...] = y.T.astype(o_ref.dtype)
    else:
        o_ref[0] = y.astype(o_ref.dtype)


def dwconv3(u, col0, w, b, silu, transpose_out=False):
    bsz, n, _ = u.shape
    ch = w.shape[1]
    tc = 512
    assert ch % tc == 0 and col0 % tc == 0 and w.shape[0] == 3
    if transpose_out:
        out_spec = pl.BlockSpec((tc, n), lambda b_, j: (j, b_))
        out_shape = jax.ShapeDtypeStruct((ch, bsz * n), F32)
    else:
        out_spec = pl.BlockSpec((1, n, tc), lambda b_, j: (b_, 0, j))
        out_shape = jax.ShapeDtypeStruct((bsz, n, ch), F32)
    out = pl.pallas_call(
        functools.partial(_dwconv_kernel, silu=silu, transpose_out=transpose_out),
        grid=(bsz, ch // tc),
        in_specs=[pl.BlockSpec((1, n, tc), lambda b_, j: (b_, 0, col0 // tc + j)),
                  pl.BlockSpec((3, tc), lambda b_, j: (0, j)),
                  pl.BlockSpec((1, tc), lambda b_, j: (0, j))],
        out_specs=out_spec,
        out_shape=out_shape,
        compiler_params=_params(2),
        name="dwconv3",
    )(u, w, b.reshape(1, ch))
    return out.reshape(ch, bsz, n) if transpose_out else out


def _hyena_filter_kernel(feats_ref, w1_ref, b1_ref, w2_ref, b2_ref, fr_ref, w3_ref, dl_ref, o_ref, h_scr):
    @pl.when(pl.program_id(0) == 0)
    def _():
        h = jnp.dot(w1_ref[...], feats_ref[...], precision=HIGHEST, preferred_element_type=F32) + b1_ref[...]
        h = jnp.sin(fr_ref[:, 0:1] * h)
        h = jnp.dot(w2_ref[...], h, precision=HIGHEST, preferred_element_type=F32) + b2_ref[...]
        h_scr[...] = jnp.sin(fr_ref[:, 1:2] * h)

    h = jnp.dot(w3_ref[...], h_scr[...], precision=HIGHEST, preferred_element_type=F32)
    o_ref[...] = h * jnp.exp(-dl_ref[...] * feats_ref[0:1, :])


def hyena_filters(n, d, w1, b1, w2, b2, w3, freq):
    t = jnp.linspace(0.0, 1.0, n, dtype=F32)[None, :]
    w = 2.0 * math.pi * jnp.arange(n, dtype=F32)[None, :] / n
    bands = jnp.linspace(1e-4, HY_BANDS - 1, HY_BANDS, dtype=F32)[:, None]
    feats = jnp.concatenate([t, jnp.cos(bands * w), -jnp.sin(bands * w)], axis=0)
    deltas = jnp.abs(jnp.linspace(math.log(HY_TARGET) / HY_DECAY_SHORT,
                                  math.log(HY_TARGET) / HY_DECAY_LONG, d, dtype=F32))
    n_f = 2 * HY_ORDER * d
    dl = jnp.tile(deltas, 2 * HY_ORDER).reshape(n_f, 1)
    hid = w1.shape[1]
    emb = -(-feats.shape[0] // 8) * 8
    feats = jnp.pad(feats, ((0, emb - feats.shape[0]), (0, 0)))
    w1 = jnp.pad(w1, ((0, emb - w1.shape[0]), (0, 0)))
    tr = 512
    assert n_f % tr == 0
    const = lambda shape: pl.BlockSpec(shape, lambda j: (0, 0))
    return pl.pallas_call(
        _hyena_filter_kernel,
        grid=(n_f // tr,),
        in_specs=[const((emb, n)), const((hid, emb)), const((hid, 1)), const((hid, hid)), const((hid, 1)),
                  const((hid, 2)),
                  pl.BlockSpec((tr, hid), lambda j: (j, 0)),
                  pl.BlockSpec((tr, 1), lambda j: (j, 0))],
        out_specs=pl.BlockSpec((tr, n), lambda j: (j, 0)),
        out_shape=jax.ShapeDtypeStruct((n_f, n), F32),
        scratch_shapes=[pltpu.VMEM((hid, n), F32)],
        compiler_params=_params(1),
        name="hyena_filter",
    )(feats, w1.T, b1.reshape(hid, 1), w2.T, b2.reshape(hid, 1), freq.T, w3.T, dl)


def _toeplitz_conv(z, gline):
    bsz, n = z.shape
    half, t = HY_TILE, 2 * HY_TILE
    m = n // t
    g = pltpu.roll(jnp.broadcast_to(gline, (half, 2 * n)), 0, 1, stride=1, stride_axis=0).astype(BF16)
    zb = z.astype(BF16)
    zr = jnp.concatenate([zb[:, j * t:(j + 1) * t] for j in range(m)], axis=0)
    acc = [None] * m
    for dd in range(-(m - 1), m):
        lo = n + t * dd
        tile = jnp.concatenate([g[:, lo:lo + t], g[:, lo - half:lo - half + t]], axis=0)
        j0, cnt = max(0, -dd), m - abs(dd)
        out = jnp.dot(zr[j0 * bsz:(j0 + cnt) * bsz], tile, preferred_element_type=F32)
        for k in range(cnt):
            i = j0 + dd + k
            piece = out[k * bsz:(k + 1) * bsz]
            acc[i] = piece if acc[i] is None else acc[i] + piece
    return jnp.concatenate(acc, axis=1)


def _hyena_conv_kernel(v_ref, x1_ref, x2_ref, g1_ref, g2_ref, sk_ref, o_ref):
    def body(c, carry):
        v = v_ref[c]
        z = x1_ref[c] * (_toeplitz_conv(v, g1_ref[pl.ds(c, 1), :]) + v * sk_ref[pl.ds(c, 1), 0:1])
        z = x2_ref[c] * (_toeplitz_conv(z, g2_ref[pl.ds(c, 1), :]) + z * sk_ref[pl.ds(c, 1), 1:2])
        o_ref[c] = z.astype(o_ref.dtype)
        return carry

    lax.fori_loop(0, v_ref.shape[0], body, 0, unroll=4)


def hyena_mixer(u, conv_w, conv_b, f_w1, f_b1, f_w2, f_b2, f_w3, f_freq, skip):
    bsz, n, d3 = u.shape
    d = d3 // (HY_ORDER + 1)
    dc = HY_CH_BLOCK
    assert d % dc == 0 and n % (2 * HY_TILE) == 0
    uc = dwconv3(u, 0, conv_w, conv_b, silu=False, transpose_out=True)
    filt = hyena_filters(n, d, f_w1, f_b1, f_w2, f_b2, f_w3, f_freq).reshape(2 * HY_ORDER, d, n)

    def gline(fwd, bwd):
        return jnp.concatenate([jnp.zeros((d, 1), F32), jnp.flip(bwd[:, 1:], axis=1), fwd], axis=1)

    g1, g2 = gline(filt[0], filt[1]), gline(filt[2], filt[3])
    nb = d // dc
    z = pl.pallas_call(
        _hyena_conv_kernel,
        grid=(nb,),
        in_specs=[pl.BlockSpec((dc, bsz, n), lambda j: (j, 0, 0)),
                  pl.BlockSpec((dc, bsz, n), lambda j: (nb + j, 0, 0)),
                  pl.BlockSpec((dc, bsz, n), lambda j: (2 * nb + j, 0, 0)),
                  pl.BlockSpec((dc, 2 * n), lambda j: (j, 0)),
                  pl.BlockSpec((dc, 2 * n), lambda j: (j, 0)),
                  pl.BlockSpec((dc, HY_ORDER), lambda j: (j, 0))],
        out_specs=pl.BlockSpec((dc, bsz, n), lambda j: (j, 0, 0)),
        out_shape=jax.ShapeDtypeStruct((d, bsz, n), BF16),
        compiler_params=_params(1),
        name="hyena_conv",
    )(uc, uc, uc, g1, g2, skip.T)
    return z.transpose(1, 2, 0)


def _softplus(x):
    return jnp.maximum(x, 0.0) + jnp.log(1.0 + jnp.exp(-jnp.abs(x)))


def _ssd_kernel(*refs, reverse, epilogue, heads_per_group):
    hp_dim, ns, q = SSM_HEAD_DIM, SSM_D_STATE, SSM_CHUNK
    if epilogue:
        (xs_ref, bc_ref, dt_ref, dtt_ref, dtb_ref, dtbt_ref, al_ref, alt_ref, s0_ref,
         yp_ref, z_ref, dsk_ref, ng_ref, y_ref, sf_ref, s_scr) = refs
    else:
        (xs_ref, bc_ref, dt_ref, dtt_ref, dtb_ref, dtbt_ref, al_ref, alt_ref, s0_ref,
         y_ref, sf_ref, s_scr) = refs
    c = pl.program_id(1)
    n_groups = s_scr.shape[0]
    n_heads = n_groups * heads_per_group
    gw = heads_per_group * hp_dim

    @pl.when(c == 0)
    def _():
        s_scr[...] = s0_ref[0]

    li = lax.broadcasted_iota(jnp.int32, (q, q), 0)
    si = lax.broadcasted_iota(jnp.int32, (q, q), 1)
    keep = (si >= li) if reverse else (si <= li)
    tri = keep.astype(F32)
    tri_t = ((li >= si) if reverse else (li <= si)).astype(F32)
    edge = 0 if reverse else q - 1

    dt = _softplus(dt_ref[0] + dtb_ref[...])
    da = dt * -jnp.exp(al_ref[...])
    a_cs = jnp.dot(tri, da, precision=HIGHEST, preferred_element_type=F32)
    da_t = _softplus(dtt_ref[0] + dtbt_ref[...]) * -jnp.exp(alt_ref[...])
    a_cs_t = jnp.dot(da_t, tri_t, precision=HIGHEST, preferred_element_type=F32)

    eh = lax.broadcasted_iota(jnp.int32, (n_heads, n_heads * hp_dim), 0)
    el = lax.broadcasted_iota(jnp.int32, (n_heads, n_heads * hp_dim), 1)
    expand_p = (el // hp_dim == eh).astype(BF16)
    eh2 = lax.broadcasted_iota(jnp.int32, (n_heads, n_heads * q), 0)
    el2 = lax.broadcasted_iota(jnp.int32, (n_heads, n_heads * q), 1)
    expand_q = (el2 // q == eh2).astype(BF16)

    def spread(v, e):
        return jnp.dot(v.astype(BF16), e, preferred_element_type=F32)

    a_hi = a_cs.astype(BF16)
    a_r1 = a_cs - a_hi.astype(F32)
    a_mid = a_r1.astype(BF16)
    a_lo = a_r1 - a_mid.astype(F32)
    acs_tile = spread(a_hi, expand_q) + spread(a_mid, expand_q) + spread(a_lo, expand_q)
    tot = a_cs[edge:edge + 1, :]
    dt_full = spread(dt, expand_p)
    st_full = spread(jnp.exp(tot - a_cs), expand_p)
    exp_a = spread(jnp.exp(a_cs), expand_p)
    tot_full = jnp.dot(jnp.broadcast_to(tot, (8, n_heads)), expand_p.astype(F32), precision=HIGHEST,
                       preferred_element_type=F32)[0:1, :]

    xs = xs_ref[0]
    x_dt = xs * dt_full
    x_in = x_dt.astype(BF16)
    x_st = (x_dt * st_full).astype(BF16)
    lane = lax.broadcasted_iota(jnp.int32, (q, 2 * hp_dim), 1)
    first = lane < hp_dim
    gn = n_groups * ns
    ys = []
    for g in range(n_groups):
        b_g = bc_ref[0, :, g * ns:(g + 1) * ns]
        c_g = bc_ref[0, :, gn + g * ns:gn + (g + 1) * ns].astype(BF16)
        cb = lax.dot_general(c_g, b_g.astype(BF16), NT_DIMS, preferred_element_type=F32)
        s_prev = s_scr[g]
        y_off = jnp.dot(c_g, s_prev.astype(BF16), preferred_element_type=F32)
        gl = slice(g * gw, (g + 1) * gw)
        s_new = jnp.dot(b_g.T.astype(BF16), x_st[:, gl], preferred_element_type=F32)
        s_scr[g] = jnp.exp(tot_full[:, gl]) * s_prev + s_new
        for e in range(0, heads_per_group, 2):
            ms = []
            for h in (g * heads_per_group + e, g * heads_per_group + e + 1):
                diff = acs_tile[:, h * q:(h + 1) * q] - a_cs_t[h:h + 1, :]
                lmat = jnp.where(keep, jnp.exp(jnp.minimum(diff, 0.0)), 0.0)
                ms.append((cb * lmat).astype(BF16))
            pl_ = slice(g * gw + e * hp_dim, g * gw + (e + 2) * hp_dim)
            yd = jnp.dot(jnp.concatenate(ms, axis=0), x_in[:, pl_], preferred_element_type=F32)
            yd = jnp.where(first, yd[:q], yd[q:])
            y = yd + y_off[:, e * hp_dim:(e + 2) * hp_dim] * exp_a[:, pl_]
            if epilogue:
                y = y + yp_ref[0, :, pl_] + xs[:, pl_] * dsk_ref[:, pl_]
                y = y * _silu(z_ref[0, :, pl_])
            ys.append(y)

    y_all = jnp.concatenate(ys, axis=1)
    y_ref[0] = _rms(y_all, ng_ref[...]).astype(y_ref.dtype) if epilogue else y_all

    @pl.when(c == pl.num_programs(1) - 1)
    def _():
        sf_ref[0] = s_scr[...]


def ssd_scan_dir(xbc, dt, dt_bias, a_log, s0, reverse, epi=None):
    bsz, n, _ = xbc.shape
    n_heads = dt.shape[-1]
    n_groups = SSM_GROUPS
    hpg = n_heads // n_groups
    d_inner = n_heads * SSM_HEAD_DIM
    gn = n_groups * SSM_D_STATE
    q = SSM_CHUNK
    nc = n // q
    assert d_inner % (2 * gn) == 0 and hpg % 2 == 0 and n % q == 0
    cix = (lambda c: nc - 1 - c) if reverse else (lambda c: c)
    dt_t = dt.swapaxes(1, 2)
    in_specs = [pl.BlockSpec((1, q, d_inner), lambda b, c: (b, cix(c), 0)),
                pl.BlockSpec((1, q, 2 * gn), lambda b, c: (b, cix(c), d_inner // (2 * gn))),
                pl.BlockSpec((1, q, n_heads), lambda b, c: (b, cix(c), 0)),
                pl.BlockSpec((1, n_heads, q), lambda b, c: (b, 0, cix(c))),
                pl.BlockSpec((1, n_heads), lambda b, c: (0, 0)),
                pl.BlockSpec((n_heads, 1), lambda b, c: (0, 0)),
                pl.BlockSpec((1, n_heads), lambda b, c: (0, 0)),
                pl.BlockSpec((n_heads, 1), lambda b, c: (0, 0)),
                pl.BlockSpec((1,) + s0.shape[1:], lambda b, c: (b, 0, 0, 0))]
    args = [xbc, xbc, dt, dt_t, dt_bias.reshape(1, n_heads), dt_bias.reshape(n_heads, 1),
            a_log.reshape(1, n_heads), a_log.reshape(n_heads, 1), s0]
    y_dtype = F32
    if epi is not None:
        y_prev, u, d_skip, norm_g = epi
        in_specs += [pl.BlockSpec((1, q, d_inner), lambda b, c: (b, cix(c), 0)),
                     pl.BlockSpec((1, q, d_inner), lambda b, c: (b, cix(c), 0)),
                     pl.BlockSpec((1, d_inner), lambda b, c: (0, 0)),
                     pl.BlockSpec((1, d_inner), lambda b, c: (0, 0))]
        args += [y_prev, u, jnp.repeat(d_skip, SSM_HEAD_DIM).reshape(1, d_inner), norm_g.reshape(1, d_inner)]
        y_dtype = BF16
    return pl.pallas_call(
        functools.partial(_ssd_kernel, reverse=reverse, epilogue=epi is not None, heads_per_group=hpg),
        grid=(bsz, nc),
        in_specs=in_specs,
        out_specs=[pl.BlockSpec((1, q, d_inner), lambda b, c: (b, cix(c), 0)),
                   pl.BlockSpec((1,) + s0.shape[1:], lambda b, c: (b, 0, 0, 0))],
        out_shape=[jax.ShapeDtypeStruct((bsz, n, d_inner), y_dtype), jax.ShapeDtypeStruct(s0.shape, F32)],
        scratch_shapes=[pltpu.VMEM(s0.shape[1:], F32)],
        compiler_params=_params(2),
        name="ssd_bwd" if reverse else "ssd_fwd",
    )(*args)


def ssd_mixer(u_lat, u_ctx, conv_w, conv_b, dt_bias, a_log, d_skip, norm_g, with_ctx_out):
    bsz = u_lat.shape[0]
    n_heads = a_log.shape[-1]
    d_inner = n_heads * SSM_HEAD_DIM
    gn = SSM_GROUPS * SSM_D_STATE
    dt0 = 2 * d_inner + 2 * gn
    xbc_c = dwconv3(u_ctx, d_inner, conv_w, conv_b, silu=True)
    xbc_l = dwconv3(u_lat, d_inner, conv_w, conv_b, silu=True)
    zero = jnp.zeros((bsz, SSM_GROUPS, SSM_D_STATE, d_inner // SSM_GROUPS), F32)

    def dts(u, d):
        return u[:, :, dt0 + d * n_heads:dt0 + (d + 1) * n_heads]

    epi_args = (d_skip, norm_g)
    y_cf, s_f = ssd_scan_dir(xbc_c, dts(u_ctx, 0), dt_bias[0], a_log[0], zero, False)
    y_c, s_b = ssd_scan_dir(xbc_c, dts(u_ctx, 1), dt_bias[1], a_log[1], zero, True, (y_cf, u_ctx) + epi_args)
    y_lf, _ = ssd_scan_dir(xbc_l, dts(u_lat, 0), dt_bias[0], a_log[0], s_f, False)
    y_l, _ = ssd_scan_dir(xbc_l, dts(u_lat, 1), dt_bias[1], a_log[1], s_b, True, (y_lf, u_lat) + epi_args)
    return y_l, (y_c if with_ctx_out else None)


NEG_BIG = -1e30


def _split_heads(q, first):
    zero = jnp.zeros_like(q)
    return jnp.concatenate([jnp.where(first, q, zero), jnp.where(first, zero, q)], axis=0)


def _na_kernel(q_ref, k_ref, v_ref, kc_ref, vc_ref, bias_ref, o_ref, *, rows, n_pairs):
    r = pl.program_id(1)
    r0 = jnp.clip(r - NA_KH // 2, 0, rows - NA_KH)
    start = pl.multiple_of(r0 * GRID_W, GRID_W)
    band = NA_KH * GRID_W
    scale = NA_HEAD_DIM ** -0.5 * LOG2E
    lane = lax.broadcasted_iota(jnp.int32, (GRID_W, 2 * NA_HEAD_DIM), 1)
    first = lane < NA_HEAD_DIM
    outs = []
    for hp in range(n_pairs):
        cols = slice(hp * 2 * NA_HEAD_DIM, (hp + 1) * 2 * NA_HEAD_DIM)
        q2 = _split_heads((q_ref[0, :, cols].astype(F32) * scale).astype(BF16), first)
        s_lat = lax.dot_general(q2, k_ref[0, pl.ds(start, band), cols], NT_DIMS, preferred_element_type=F32)
        s_lat = s_lat + bias_ref[0, hp]
        s_ctx = lax.dot_general(q2, kc_ref[0, :, cols], NT_DIMS, preferred_element_type=F32)
        m = jnp.maximum(jnp.max(s_lat, axis=-1, keepdims=True), jnp.max(s_ctx, axis=-1, keepdims=True))
        p_lat = jnp.exp2(s_lat - m)
        p_ctx = jnp.exp2(s_ctx - m)
        l = jnp.sum(p_lat, axis=-1, keepdims=True) + jnp.sum(p_ctx, axis=-1, keepdims=True)
        o = (jnp.dot(p_lat.astype(BF16), v_ref[0, pl.ds(start, band), cols], preferred_element_type=F32)
             + jnp.dot(p_ctx.astype(BF16), vc_ref[0, :, cols], preferred_element_type=F32)) / l
        outs.append(jnp.where(first, o[:GRID_W], o[GRID_W:]).astype(o_ref.dtype))
    o_ref[0] = jnp.concatenate(outs, axis=1)


def _na_bias(rpb):
    n_heads = rpb.shape[0]
    col = jnp.arange(GRID_W)
    c0 = jnp.clip(col - NA_KW // 2, 0, GRID_W - NA_KW)
    col_ok = (col[None, :] >= c0[:, None]) & (col[None, :] < c0[:, None] + NA_KW)
    dx_idx = jnp.clip(col[None, :] - col[:, None] + NA_KW - 1, 0, 2 * NA_KW - 2)
    rpb_cols = rpb.astype(F32)[:, :, dx_idx]
    dy_idx = jnp.arange(NA_KH)[None, :] - jnp.arange(NA_KH)[:, None] + NA_KH - 1
    bias = rpb_cols[:, dy_idx]
    bias = jnp.where(col_ok[None, None, None], bias * LOG2E, NEG_BIG)
    return bias.transpose(1, 0, 3, 2, 4).reshape(NA_KH, n_heads // 2, 2 * GRID_W, NA_KH * GRID_W)


def na_attention(u_lat, u_ctx, rpb):
    bsz, n_lat, d3 = u_lat.shape
    d = d3 // 3
    n_ctx = u_ctx.shape[1]
    rows = n_lat // GRID_W
    assert rows >= NA_KH and d % (2 * NA_HEAD_DIM) == 0
    n_pairs = d // (2 * NA_HEAD_DIM)
    bias = _na_bias(rpb)

    def variant(r):
        return r - jnp.clip(r - NA_KH // 2, 0, rows - NA_KH)

    return pl.pallas_call(
        functools.partial(_na_kernel, rows=rows, n_pairs=n_pairs),
        grid=(bsz, rows),
        in_specs=[pl.BlockSpec((1, GRID_W, d), lambda b, r: (b, r, 0)),
                  pl.BlockSpec((1, n_lat, d), lambda b, r: (b, 0, 1)),
                  pl.BlockSpec((1, n_lat, d), lambda b, r: (b, 0, 2)),
                  pl.BlockSpec((1, n_ctx, d), lambda b, r: (b, 0, 1)),
                  pl.BlockSpec((1, n_ctx, d), lambda b, r: (b, 0, 2)),
                  pl.BlockSpec((1, n_pairs, 2 * GRID_W, NA_KH * GRID_W), lambda b, r: (variant(r), 0, 0, 0))],
        out_specs=pl.BlockSpec((1, GRID_W, d), lambda b, r: (b, r, 0)),
        out_shape=jax.ShapeDtypeStruct((bsz, n_lat, d), BF16),
        compiler_params=_params(2),
        name="na_attention",
    )(u_lat, u_lat, u_lat, u_ctx, u_ctx, bias)


def _mha_ctx_kernel(q_ref, k_ref, v_ref, o_ref, *, n_pairs):
    n = q_ref.shape[1]
    scale = NA_HEAD_DIM ** -0.5
    lane = lax.broadcasted_iota(jnp.int32, (n, 2 * NA_HEAD_DIM), 1)
    first = lane < NA_HEAD_DIM
    for hp in range(n_pairs):
        cols = slice(hp * 2 * NA_HEAD_DIM, (hp + 1) * 2 * NA_HEAD_DIM)
        q2 = _split_heads(q_ref[0, :, cols], first)
        s = lax.dot_general(q2, k_ref[0, :, cols], NT_DIMS, preferred_element_type=F32) * scale
        p = jnp.exp(s - jnp.max(s, axis=-1, keepdims=True))
        l = jnp.sum(p, axis=-1, keepdims=True)
        o = jnp.dot(p.astype(BF16), v_ref[0, :, cols], preferred_element_type=F32) / l
        o_ref[0, :, cols] = jnp.where(first, o[:n], o[n:]).astype(o_ref.dtype)


def mha_ctx(u_ctx):
    bsz, n, d3 = u_ctx.shape
    d = d3 // 3
    n_pairs = d // (2 * NA_HEAD_DIM)
    return pl.pallas_call(
        functools.partial(_mha_ctx_kernel, n_pairs=n_pairs),
        grid=(bsz,),
        in_specs=[pl.BlockSpec((1, n, d), lambda b: (b, 0, 0)),
                  pl.BlockSpec((1, n, d), lambda b: (b, 0, 1)),
                  pl.BlockSpec((1, n, d), lambda b: (b, 0, 2))],
        out_specs=pl.BlockSpec((1, n, d), lambda b: (b, 0, 0)),
        out_shape=jax.ShapeDtypeStruct((bsz, n, d), BF16),
        compiler_params=_params(1),
        name="mha_ctx",
    )(u_ctx, u_ctx, u_ctx)


def kernel(x, c, ctx, c_ctx, ada_w, ada_b, norm_g, mlp_w1, mlp_w2, ga_w_in, ga_q_gain, ga_k_gain, ga_w_out, hy_w_in, hy_conv_w, hy_conv_b, hy_f_w1, hy_f_b1, hy_f_w2, hy_f_b2, hy_f_w3, hy_f_freq, hy_skip, hy_w_out, ssm_w_in, ssm_conv_w, ssm_conv_b, ssm_dt_bias, ssm_a_log, ssm_d, ssm_norm_g, ssm_w_out, na_w_in, na_rpb, na_w_out):
    bsz, n_lat, d = x.shape
    n_ctx = ctx.shape[1]
    depth = ada_w.shape[0]
    sc = jnp.concatenate([jax.nn.silu(c), jax.nn.silu(c_ctx)[None, :]], axis=0)
    x_lat = x.reshape(bsz * n_lat, d)
    x_ctx = ctx.reshape(bsz * n_ctx, d)
    for i in range(depth):
        kind, j = i % N_MIXERS, i // N_MIXERS
        with_ctx = i < depth - 1
        mod = dense_f32(sc, ada_w[i], ada_b[i][None, :]).reshape(bsz + 1, 6, d)
        mod_l, mod_c = mod[:bsz], mod[bsz:]
        g = norm_g[i]
        w_in, w_out = ((ga_w_in, ga_w_out), (hy_w_in, hy_w_out), (ssm_w_in, ssm_w_out), (na_w_in, na_w_out))[kind]
        w_in, w_out = w_in[j].astype(BF16), w_out[j].astype(BF16)
        u_dtype = BF16 if kind == 3 else F32
        u_l = norm_proj(x_lat, mod_l, g, w_in, n_lat, u_dtype).reshape(bsz, n_lat, -1)
        u_c = norm_proj(x_ctx, mod_c, g, w_in, bsz * n_ctx, u_dtype).reshape(bsz, n_ctx, -1)
        if kind == 0:
            a_l = gqa_attention(u_l, u_l, u_c, ga_q_gain[j], ga_k_gain[j])
            a_c = gqa_attention(u_c, None, u_c, ga_q_gain[j], ga_k_gain[j]) if with_ctx else None
        elif kind == 1:
            hy = (hy_conv_w[j], hy_conv_b[j], hy_f_w1[j], hy_f_b1[j], hy_f_w2[j], hy_f_b2[j], hy_f_w3[j],
                  hy_f_freq[j], hy_skip[j])
            a_l = hyena_mixer(u_l, *hy)
            a_c = hyena_mixer(u_c, *hy) if with_ctx else None
        elif kind == 2:
            a_l, a_c = ssd_mixer(u_l, u_c, ssm_conv_w[j], ssm_conv_b[j], ssm_dt_bias[j], ssm_a_log[j],
                                 ssm_d[j], ssm_norm_g[j], with_ctx)
        else:
            a_l = na_attention(u_l, u_c, na_rpb[j])
            a_c = mha_ctx(u_c) if with_ctx else None
        w1, w2 = mlp_w1[i].astype(BF16), mlp_w2[i].astype(BF16)
        x_lat, h_l = out_proj_residual(a_l.reshape(bsz * n_lat, -1), w_out, x_lat, mod_l, g, n_lat)
        x_lat = mlp_residual(h_l, w1, w2, x_lat, mod_l, g, n_lat)
        if with_ctx:
            x_ctx, h_c = out_proj_residual(a_c.reshape(bsz * n_ctx, -1), w_out, x_ctx, mod_c, g, bsz * n_ctx)
            x_ctx = mlp_residual(h_c, w1, w2, x_ctx, mod_c, g, bsz * n_ctx)
    return x_lat.reshape(bsz, n_lat, d)
```

```python
import functools
import math

import jax
import jax.numpy as jnp
from jax import lax
from jax.experimental import pallas as pl
from jax.experimental.pallas import tpu as pltpu

GRID_W = 64
N_MIXERS = 4
RMS_EPS = 1e-6

GA_HEAD_DIM = 128
GA_N_KV = 2
ROPE_THETA = 10000.0

HY_ORDER = 2
HY_BANDS = 16
HY_DECAY_SHORT = 0.3
HY_DECAY_LONG = 1.5
HY_TARGET = 1e-2
HY_TILE = 128
HY_CH_BLOCK = 8

SSM_HEAD_DIM = 64
SSM_GROUPS = 4
SSM_D_STATE = 128
SSM_CHUNK = 128

NA_HEAD_DIM = 64
NA_KH = 8
NA_KW = 16

VMEM_LIMIT_BYTES = 56 * 1024 * 1024
BF16 = jnp.bfloat16
F32 = jnp.float32
HIGHEST = lax.Precision.HIGHEST
NT_DIMS = (((1,), (1,)), ((), ()))
LOG2E = 1.4426950408889634


def _params(n_grid_dims):
    return pltpu.CompilerParams(dimension_semantics=("arbitrary",) * n_grid_dims,
                                vmem_limit_bytes=VMEM_LIMIT_BYTES)


def _row_tile(rows_per_mod, cap):
    t = min(rows_per_mod, cap)
    while rows_per_mod % t:
        t //= 2
    return t


def _col_chunks(n, width):
    return [(c, min(width, n - c)) for c in range(0, n, width)]


def _rms(x, g):
    return x * lax.rsqrt(jnp.mean(x * x, axis=-1, keepdims=True) + RMS_EPS) * g


def _silu(x):
    return x / (1.0 + jnp.exp(-x))


def _dense_kernel(a_ref, w_ref, b_ref, o_ref):
    o_ref[...] = jnp.dot(a_ref[...], w_ref[...], preferred_element_type=F32, precision=HIGHEST) + b_ref[...]


def dense_f32(a, w, b):
    m, k = a.shape
    n = w.shape[1]
    tn = 1024 if n % 1024 == 0 else n
    return pl.pallas_call(
        _dense_kernel,
        grid=(n // tn,),
        in_specs=[pl.BlockSpec((m, k), lambda j: (0, 0)),
                  pl.BlockSpec((k, tn), lambda j: (0, j)),
                  pl.BlockSpec((1, tn), lambda j: (0, j))],
        out_specs=pl.BlockSpec((m, tn), lambda j: (0, j)),
        out_shape=jax.ShapeDtypeStruct((m, n), F32),
        compiler_params=_params(1),
        name="ada_dense",
    )(a, w, b)


def _norm_proj_kernel(x_ref, mod_ref, g_ref, w_ref, o_ref, *, tn):
    x = x_ref[...]
    h = _rms(x, g_ref[0:1, :]) * (1.0 + mod_ref[0, 1:2, :]) + mod_ref[0, 0:1, :]
    h = h.astype(BF16)
    for c, s in _col_chunks(w_ref.shape[1], tn):
        o_ref[:, c:c + s] = jnp.dot(h, w_ref[:, c:c + s], preferred_element_type=F32).astype(o_ref.dtype)


def norm_proj(x, mod, g, w, rows_per_mod, out_dtype=F32):
    m, d = x.shape
    n = w.shape[1]
    tm = _row_tile(rows_per_mod, 512)
    bpm = rows_per_mod // tm
    return pl.pallas_call(
        functools.partial(_norm_proj_kernel, tn=512),
        grid=(m // tm,),
        in_specs=[pl.BlockSpec((tm, d), lambda i: (i, 0)),
                  pl.BlockSpec((1, 6, d), lambda i: (i // bpm, 0, 0)),
                  pl.BlockSpec((4, d), lambda i: (0, 0)),
                  pl.BlockSpec((d, n), lambda i: (0, 0))],
        out_specs=pl.BlockSpec((tm, n), lambda i: (i, 0)),
        out_shape=jax.ShapeDtypeStruct((m, n), out_dtype),
        compiler_params=_params(1),
        name="norm_proj",
    )(x, mod, g, w)


def _out_proj_kernel(a_ref, w_ref, x_ref, mod_ref, g_ref, xo_ref, h_ref):
    y = jnp.dot(a_ref[...].astype(BF16), w_ref[...], preferred_element_type=F32)
    x = x_ref[...] + mod_ref[0, 2:3, :] * _rms(y, g_ref[1:2, :])
    xo_ref[...] = x
    h = _rms(x, g_ref[2:3, :]) * (1.0 + mod_ref[0, 4:5, :]) + mod_ref[0, 3:4, :]
    h_ref[...] = h.astype(BF16)


def out_proj_residual(a, w, x, mod, g, rows_per_mod):
    m, k = a.shape
    d = w.shape[1]
    tm = _row_tile(rows_per_mod, 512)
    bpm = rows_per_mod // tm
    return pl.pallas_call(
        _out_proj_kernel,
        grid=(m // tm,),
        in_specs=[pl.BlockSpec((tm, k), lambda i: (i, 0)),
                  pl.BlockSpec((k, d), lambda i: (0, 0)),
                  pl.BlockSpec((tm, d), lambda i: (i, 0)),
                  pl.BlockSpec((1, 6, d), lambda i: (i // bpm, 0, 0)),
                  pl.BlockSpec((4, d), lambda i: (0, 0))],
        out_specs=[pl.BlockSpec((tm, d), lambda i: (i, 0)),
                   pl.BlockSpec((tm, d), lambda i: (i, 0))],
        out_shape=[jax.ShapeDtypeStruct((m, d), F32), jax.ShapeDtypeStruct((m, d), BF16)],
        compiler_params=_params(1),
        name="out_proj",
    )(a, w, x, mod, g)


def _mlp_kernel(h_ref, w1_ref, w2_ref, x_ref, mod_ref, g_ref, o_ref, *, fc):
    h = h_ref[...]
    acc = jnp.zeros(o_ref.shape, F32)
    for c, s in _col_chunks(w1_ref.shape[1], fc):
        u = jnp.dot(h, w1_ref[:, c:c + s], preferred_element_type=F32)
        u = jnp.square(jnp.maximum(u, 0.0)).astype(BF16)
        acc = acc + jnp.dot(u, w2_ref[c:c + s, :], preferred_element_type=F32)
    o_ref[...] = x_ref[...] + mod_ref[0, 5:6, :] * _rms(acc, g_ref[3:4, :])


def mlp_residual(h, w1, w2, x, mod, g, rows_per_mod):
    m, d = x.shape
    f = w1.shape[1]
    tm = _row_tile(rows_per_mod, 512)
    bpm = rows_per_mod // tm
    return pl.pallas_call(
        functools.partial(_mlp_kernel, fc=1024),
        grid=(m // tm,),
        in_specs=[pl.BlockSpec((tm, d), lambda i: (i, 0)),
                  pl.BlockSpec((d, f), lambda i: (0, 0)),
                  pl.BlockSpec((f, d), lambda i: (0, 0)),
                  pl.BlockSpec((tm, d), lambda i: (i, 0)),
                  pl.BlockSpec((1, 6, d), lambda i: (i // bpm, 0, 0)),
                  pl.BlockSpec((4, d), lambda i: (0, 0))],
        out_specs=pl.BlockSpec((tm, d), lambda i: (i, 0)),
        out_shape=jax.ShapeDtypeStruct((m, d), F32),
        compiler_params=_params(1),
        name="mlp",
    )(h, w1, w2, x, mod, g)


def _proj_mlp_kernel(a_ref, wo_ref, w1_ref, w2_ref, x_ref, mod_ref, g_ref, o_ref, *, fc):
    y = jnp.dot(a_ref[...].astype(BF16), wo_ref[...], preferred_element_type=F32)
    x = x_ref[...] + mod_ref[0, 2:3, :] * _rms(y, g_ref[1:2, :])
    h = (_rms(x, g_ref[2:3, :]) * (1.0 + mod_ref[0, 4:5, :]) + mod_ref[0, 3:4, :]).astype(BF16)
    acc = jnp.zeros(o_ref.shape, F32)
    for c, s in _col_chunks(w1_ref.shape[1], fc):
        u = jnp.dot(h, w1_ref[:, c:c + s], preferred_element_type=F32)
        u = jnp.square(jnp.maximum(u, 0.0)).astype(BF16)
        acc = acc + jnp.dot(u, w2_ref[c:c + s, :], preferred_element_type=F32)
    o_ref[...] = x + mod_ref[0, 5:6, :] * _rms(acc, g_ref[3:4, :])


def proj_mlp_residual(a, wo, w1, w2, x, mod, g, rows_per_mod):
    m, k = a.shape
    d = wo.shape[1]
    f = w1.shape[1]
    tm = _row_tile(rows_per_mod, 512)
    bpm = rows_per_mod // tm
    resident = lambda shape: pl.BlockSpec(shape, lambda i: (0, 0), pipeline_mode=pl.Buffered(1))
    return pl.pallas_call(
        functools.partial(_proj_mlp_kernel, fc=1024),
        grid=(m // tm,),
        in_specs=[pl.BlockSpec((tm, k), lambda i: (i, 0)),
                  resident((k, d)), resident((d, f)), resident((f, d)),
                  pl.BlockSpec((tm, d), lambda i: (i, 0)),
                  pl.BlockSpec((1, 6, d), lambda i: (i // bpm, 0, 0)),
                  pl.BlockSpec((4, d), lambda i: (0, 0))],
        out_specs=pl.BlockSpec((tm, d), lambda i: (i, 0)),
        out_shape=jax.ShapeDtypeStruct((m, d), F32),
        compiler_params=_params(1),
        name="proj_mlp",
    )(a, wo, w1, w2, x, mod, g)


def _rope(x, cos, sin_lo, sin_hi):
    return x * cos + pltpu.roll(x, 96, 1) * sin_lo + pltpu.roll(x, 32, 1) * sin_hi


def _gqa_kernel(*refs, n_lat, group):
    hd = GA_HEAD_DIM
    if n_lat:
        (q_ref, kl_ref, vl_ref, kc_ref, vc_ref, qg_ref, kg_ref, tq_ref, tk_ref, o_ref, k_scr, v_scr) = refs
    else:
        (q_ref, kc_ref, vc_ref, qg_ref, kg_ref, o_ref, k_scr, v_scr) = refs

    @pl.when(pl.program_id(2) == 0)
    def _():
        if n_lat:
            k = _rope(_rms(kl_ref[0], kg_ref[...]), tk_ref[0], tk_ref[1], tk_ref[2])
            k_scr[0:n_lat, :] = k.astype(BF16)
            v_scr[0:n_lat, :] = vl_ref[0].astype(BF16)
        k_scr[n_lat:, :] = _rms(kc_ref[0], kg_ref[...]).astype(BF16)
        v_scr[n_lat:, :] = vc_ref[0].astype(BF16)

    outs = []
    for h in range(group):
        q = _rms(q_ref[0, :, h * hd:(h + 1) * hd], qg_ref[...])
        if n_lat:
            q = _rope(q, tq_ref[0], tq_ref[1], tq_ref[2])
        q = (q * (hd ** -0.5 * LOG2E)).astype(BF16)
        s = lax.dot_general(q, k_scr[...], NT_DIMS, preferred_element_type=F32)
        p = jnp.exp2(s - jnp.max(s, axis=-1, keepdims=True))
        l = jnp.sum(p, axis=-1, keepdims=True)
        o = jnp.dot(p.astype(BF16), v_scr[...], preferred_element_type=F32) / l
        outs.append(o.astype(o_ref.dtype))
    o_ref[0] = jnp.concatenate(outs, axis=1)


def _rope_tables(n_tokens, head_dim):
    t = jnp.arange(n_tokens)
    row = (t // GRID_W).astype(F32)
    col = (t % GRID_W).astype(F32)
    nf = head_dim // 4
    inv = ROPE_THETA ** (-jnp.arange(nf, dtype=F32) / nf)
    ar = row[:, None] * inv[None, :]
    ac = col[:, None] * inv[None, :]
    zero = jnp.zeros_like(ar)
    cos = jnp.concatenate([jnp.cos(ar), jnp.cos(ar), jnp.cos(ac), jnp.cos(ac)], axis=-1)
    sin_lo = jnp.concatenate([-jnp.sin(ar), zero, -jnp.sin(ac), zero], axis=-1)
    sin_hi = jnp.concatenate([zero, jnp.sin(ar), zero, jnp.sin(ac)], axis=-1)
    return jnp.stack([cos, sin_lo, sin_hi])


def gqa_attention(u_q, u_lat, u_ctx, q_gain, k_gain):
    hd = GA_HEAD_DIM
    bsz, nq, width = u_q.shape
    n_heads = width // hd - 2 * GA_N_KV
    group = n_heads // GA_N_KV
    n_lat = 0 if u_lat is None else u_lat.shape[1]
    n_ctx = u_ctx.shape[1]
    tq = _row_tile(nq, 256)
    k_col, v_col = n_heads, n_heads + GA_N_KV
    qg, kg = q_gain.reshape(1, hd), k_gain.reshape(1, hd)
    q_spec = pl.BlockSpec((1, tq, group * hd), lambda b, kv, i: (b, i, kv))
    ctx_specs = [pl.BlockSpec((1, n_ctx, hd), lambda b, kv, i: (b, 0, k_col + kv)),
                 pl.BlockSpec((1, n_ctx, hd), lambda b, kv, i: (b, 0, v_col + kv))]
    gain_specs = [pl.BlockSpec((1, hd), lambda b, kv, i: (0, 0))] * 2
    if n_lat:
        tables = _rope_tables(n_lat, hd)
        in_specs = ([q_spec,
                     pl.BlockSpec((1, n_lat, hd), lambda b, kv, i: (b, 0, k_col + kv)),
                     pl.BlockSpec((1, n_lat, hd), lambda b, kv, i: (b, 0, v_col + kv))]
                    + ctx_specs + gain_specs
                    + [pl.BlockSpec((3, tq, hd), lambda b, kv, i: (0, i, 0)),
                       pl.BlockSpec((3, n_lat, hd), lambda b, kv, i: (0, 0, 0))])
        args = (u_q, u_lat, u_lat, u_ctx, u_ctx, qg, kg, tables, tables)
    else:
        in_specs = [q_spec] + ctx_specs + gain_specs
        args = (u_q, u_ctx, u_ctx, qg, kg)
    return pl.pallas_call(
        functools.partial(_gqa_kernel, n_lat=n_lat, group=group),
        grid=(bsz, GA_N_KV, nq // tq),
        in_specs=in_specs,
        out_specs=pl.BlockSpec((1, tq, group * hd), lambda b, kv, i: (b, i, kv)),
        out_shape=jax.ShapeDtypeStruct((bsz, nq, n_heads * hd), BF16),
        scratch_shapes=[pltpu.VMEM((n_lat + n_ctx, hd), BF16), pltpu.VMEM((n_lat + n_ctx, hd), BF16)],
        compiler_params=_params(3),
        name="gqa_lat" if n_lat else "gqa_ctx",
    )(*args)


def _dwconv_kernel(u_ref, w_ref, b_ref, o_ref, *, silu, transpose_out):
    x = u_ref[0]
    n = x.shape[0]
    row = lax.broadcasted_iota(jnp.int32, x.shape, 0)
    prev = jnp.where(row == 0, 0.0, pltpu.roll(x, 1, 0))
    nxt = jnp.where(row == n - 1, 0.0, pltpu.roll(x, n - 1, 0))
    y = prev * w_ref[0:1, :] + x * w_ref[1:2, :] + nxt * w_ref[2:3, :] + b_ref[...]
    if silu:
        y = _silu(y)
    if transpose_out:
        o_ref[...] = y.T.astype(o_ref.dtype)
    else:
        o_ref[0] = y.astype(o_ref.dtype)


def dwconv3(u, col0, w, b, silu, transpose_out=False):
    bsz, n, _ = u.shape
    ch = w.shape[1]
    tc = 512
    assert ch % tc == 0 and col0 % tc == 0 and w.shape[0] == 3
    if transpose_out:
        out_spec = pl.BlockSpec((tc, n), lambda b_, j: (j, b_))
        out_shape = jax.ShapeDtypeStruct((ch, bsz * n), F32)
    else:
        out_spec = pl.BlockSpec((1, n, tc), lambda b_, j: (b_, 0, j))
        out_shape = jax.ShapeDtypeStruct((bsz, n, ch), F32)
    out = pl.pallas_call(
        functools.partial(_dwconv_kernel, silu=silu, transpose_out=transpose_out),
        grid=(bsz, ch // tc),
        in_specs=[pl.BlockSpec((1, n, tc), lambda b_, j: (b_, 0, col0 // tc + j)),
                  pl.BlockSpec((3, tc), lambda b_, j: (0, j)),
                  pl.BlockSpec((1, tc), lambda b_, j: (0, j))],
        out_specs=out_spec,
        out_shape=out_shape,
        compiler_params=_params(2),
        name="dwconv3",
    )(u, w, b.reshape(1, ch))
    return out.reshape(ch, bsz, n) if transpose_out else out


def _hyena_filter_kernel(feats_ref, w1_ref, b1_ref, w2_ref, b2_ref, fr_ref, w3_ref, dl_ref, o_ref, h_scr):
    @pl.when(pl.program_id(0) == 0)
    def _():
        h = jnp.dot(w1_ref[...], feats_ref[...], precision=HIGHEST, preferred_element_type=F32) + b1_ref[...]
        h = jnp.sin(fr_ref[:, 0:1] * h)
        h = jnp.dot(w2_ref[...], h, precision=HIGHEST, preferred_element_type=F32) + b2_ref[...]
        h_scr[...] = jnp.sin(fr_ref[:, 1:2] * h)

    h = jnp.dot(w3_ref[...], h_scr[...], precision=HIGHEST, preferred_element_type=F32)
    o_ref[...] = h * jnp.exp(-dl_ref[...] * feats_ref[0:1, :])


def hyena_filters(n, d, w1, b1, w2, b2, w3, freq):
    t = jnp.linspace(0.0, 1.0, n, dtype=F32)[None, :]
    w = 2.0 * math.pi * jnp.arange(n, dtype=F32)[None, :] / n
    bands = jnp.linspace(1e-4, HY_BANDS - 1, HY_BANDS, dtype=F32)[:, None]
    feats = jnp.concatenate([t, jnp.cos(bands * w), -jnp.sin(bands * w)], axis=0)
    deltas = jnp.abs(jnp.linspace(math.log(HY_TARGET) / HY_DECAY_SHORT,
                                  math.log(HY_TARGET) / HY_DECAY_LONG, d, dtype=F32))
    n_f = 2 * HY_ORDER * d
    dl = jnp.tile(deltas, 2 * HY_ORDER).reshape(n_f, 1)
    hid = w1.shape[1]
    emb = -(-feats.shape[0] // 8) * 8
    feats = jnp.pad(feats, ((0, emb - feats.shape[0]), (0, 0)))
    w1 = jnp.pad(w1, ((0, emb - w1.shape[0]), (0, 0)))
    tr = 512
    assert n_f % tr == 0
    const = lambda shape: pl.BlockSpec(shape, lambda j: (0, 0))
    return pl.pallas_call(
        _hyena_filter_kernel,
        grid=(n_f // tr,),
        in_specs=[const((emb, n)), const((hid, emb)), const((hid, 1)), const((hid, hid)), const((hid, 1)),
                  const((hid, 2)),
                  pl.BlockSpec((tr, hid), lambda j: (j, 0)),
                  pl.BlockSpec((tr, 1), lambda j: (j, 0))],
        out_specs=pl.BlockSpec((tr, n), lambda j: (j, 0)),
        out_shape=jax.ShapeDtypeStruct((n_f, n), F32),
        scratch_shapes=[pltpu.VMEM((hid, n), F32)],
        compiler_params=_params(1),
        name="hyena_filter",
    )(feats, w1.T, b1.reshape(hid, 1), w2.T, b2.reshape(hid, 1), freq.T, w3.T, dl)


def _toeplitz_conv(z, gline):
    bsz, n = z.shape
    half, t = HY_TILE, 2 * HY_TILE
    m = n // t
    g = pltpu.roll(jnp.broadcast_to(gline, (half, 2 * n)), 0, 1, stride=1, stride_axis=0).astype(BF16)
    zb = z.astype(BF16)
    zr = jnp.concatenate([zb[:, j * t:(j + 1) * t] for j in range(m)], axis=0)
    acc = [None] * m
    for dd in range(-(m - 1), m):
        lo = n + t * dd
        tile = jnp.concatenate([g[:, lo:lo + t], g[:, lo - half:lo - half + t]], axis=0)
        j0, cnt = max(0, -dd), m - abs(dd)
        out = jnp.dot(zr[j0 * bsz:(j0 + cnt) * bsz], tile, preferred_element_type=F32)
        for k in range(cnt):
            i = j0 + dd + k
            piece = out[k * bsz:(k + 1) * bsz]
            acc[i] = piece if acc[i] is None else acc[i] + piece
    return jnp.concatenate(acc, axis=1)


def _hyena_conv_kernel(v_ref, x1_ref, x2_ref, g1_ref, g2_ref, sk_ref, o_ref):
    def body(c, carry):
        v = v_ref[c]
        z = x1_ref[c] * (_toeplitz_conv(v, g1_ref[pl.ds(c, 1), :]) + v * sk_ref[pl.ds(c, 1), 0:1])
        z = x2_ref[c] * (_toeplitz_conv(z, g2_ref[pl.ds(c, 1), :]) + z * sk_ref[pl.ds(c, 1), 1:2])
        o_ref[c] = z.astype(o_ref.dtype)
        return carry

    lax.fori_loop(0, v_ref.shape[0], body, 0, unroll=4)


def hyena_mixer(u, conv_w, conv_b, f_w1, f_b1, f_w2, f_b2, f_w3, f_freq, skip):
    bsz, n, d3 = u.shape
    d = d3 // (HY_ORDER + 1)
    dc = HY_CH_BLOCK
    assert d % dc == 0 and n % (2 * HY_TILE) == 0
    uc = dwconv3(u, 0, conv_w, conv_b, silu=False, transpose_out=True)
    filt = hyena_filters(n, d, f_w1, f_b1, f_w2, f_b2, f_w3, f_freq).reshape(2 * HY_ORDER, d, n)

    def gline(fwd, bwd):
        return jnp.concatenate([jnp.zeros((d, 1), F32), jnp.flip(bwd[:, 1:], axis=1), fwd], axis=1)

    g1, g2 = gline(filt[0], filt[1]), gline(filt[2], filt[3])
    nb = d // dc
    z = pl.pallas_call(
        _hyena_conv_kernel,
        grid=(nb,),
        in_specs=[pl.BlockSpec((dc, bsz, n), lambda j: (j, 0, 0)),
                  pl.BlockSpec((dc, bsz, n), lambda j: (nb + j, 0, 0)),
                  pl.BlockSpec((dc, bsz, n), lambda j: (2 * nb + j, 0, 0)),
                  pl.BlockSpec((dc, 2 * n), lambda j: (j, 0)),
                  pl.BlockSpec((dc, 2 * n), lambda j: (j, 0)),
                  pl.BlockSpec((dc, HY_ORDER), lambda j: (j, 0))],
        out_specs=pl.BlockSpec((dc, bsz, n), lambda j: (j, 0, 0)),
        out_shape=jax.ShapeDtypeStruct((d, bsz, n), BF16),
        compiler_params=_params(1),
        name="hyena_conv",
    )(uc, uc, uc, g1, g2, skip.T)
    return z.transpose(1, 2, 0)


def _softplus(x):
    return jnp.maximum(x, 0.0) + jnp.log(1.0 + jnp.exp(-jnp.abs(x)))


def _ssd_kernel(*refs, reverse, epilogue, heads_per_group):
    hp_dim, ns, q = SSM_HEAD_DIM, SSM_D_STATE, SSM_CHUNK
    if epilogue:
        (xs_ref, bc_ref, dt_ref, dtt_ref, dtb_ref, dtbt_ref, al_ref, alt_ref, s0_ref,
         yp_ref, z_ref, dsk_ref, ng_ref, y_ref, sf_ref, s_scr) = refs
    else:
        (xs_ref, bc_ref, dt_ref, dtt_ref, dtb_ref, dtbt_ref, al_ref, alt_ref, s0_ref,
         y_ref, sf_ref, s_scr) = refs
    c = pl.program_id(1)
    n_groups = s_scr.shape[0]
    n_heads = n_groups * heads_per_group
    gw = heads_per_group * hp_dim

    @pl.when(c == 0)
    def _():
        s_scr[...] = s0_ref[0]

    li = lax.broadcasted_iota(jnp.int32, (q, q), 0)
    si = lax.broadcasted_iota(jnp.int32, (q, q), 1)
    keep = (si >= li) if reverse else (si <= li)
    tri = keep.astype(F32)
    tri_t = ((li >= si) if reverse else (li <= si)).astype(F32)
    edge = 0 if reverse else q - 1

    dt = _softplus(dt_ref[0] + dtb_ref[...])
    da = dt * -jnp.exp(al_ref[...])
    a_cs = jnp.dot(tri, da, precision=HIGHEST, preferred_element_type=F32)
    da_t = _softplus(dtt_ref[0] + dtbt_ref[...]) * -jnp.exp(alt_ref[...])
    a_cs_t = jnp.dot(da_t, tri_t, precision=HIGHEST, preferred_element_type=F32)

    eh = lax.broadcasted_iota(jnp.int32, (n_heads, n_heads * hp_dim), 0)
    el = lax.broadcasted_iota(jnp.int32, (n_heads, n_heads * hp_dim), 1)
    expand_p = (el // hp_dim == eh).astype(BF16)
    eh2 = lax.broadcasted_iota(jnp.int32, (n_heads, n_heads * q), 0)
    el2 = lax.broadcasted_iota(jnp.int32, (n_heads, n_heads * q), 1)
    expand_q = (el2 // q == eh2).astype(BF16)

    def spread(v, e):
        return jnp.dot(v.astype(BF16), e, preferred_element_type=F32)

    a_hi = a_cs.astype(BF16)
    a_r1 = a_cs - a_hi.astype(F32)
    a_mid = a_r1.astype(BF16)
    a_lo = a_r1 - a_mid.astype(F32)
    acs_tile = spread(a_hi, expand_q) + spread(a_mid, expand_q) + spread(a_lo, expand_q)
    tot = a_cs[edge:edge + 1, :]
    dt_full = spread(dt, expand_p)
    st_full = spread(jnp.exp(tot - a_cs), expand_p)
    exp_a = spread(jnp.exp(a_cs), expand_p)
    tot_full = jnp.dot(jnp.broadcast_to(tot, (8, n_heads)), expand_p.astype(F32), precision=HIGHEST,
                       preferred_element_type=F32)[0:1, :]

    xs = xs_ref[0]
    x_dt = xs * dt_full
    x_in = x_dt.astype(BF16)
    x_st = (x_dt * st_full).astype(BF16)
    lane = lax.broadcasted_iota(jnp.int32, (q, 2 * hp_dim), 1)
    first = lane < hp_dim
    gn = n_groups * ns
    ys = []
    for g in range(n_groups):
        b_g = bc_ref[0, :, g * ns:(g + 1) * ns]
        c_g = bc_ref[0, :, gn + g * ns:gn + (g + 1) * ns].astype(BF16)
        cb = lax.dot_general(c_g, b_g.astype(BF16), NT_DIMS, preferred_element_type=F32)
        s_prev = s_scr[g]
        y_off = jnp.dot(c_g, s_prev.astype(BF16), preferred_element_type=F32)
        gl = slice(g * gw, (g + 1) * gw)
        s_new = jnp.dot(b_g.T.astype(BF16), x_st[:, gl], preferred_element_type=F32)
        s_scr[g] = jnp.exp(tot_full[:, gl]) * s_prev + s_new
        for e in range(0, heads_per_group, 2):
            ms = []
            for h in (g * heads_per_group + e, g * heads_per_group + e + 1):
                diff = acs_tile[:, h * q:(h + 1) * q] - a_cs_t[h:h + 1, :]
                lmat = jnp.where(keep, jnp.exp(jnp.minimum(diff, 0.0)), 0.0)
                ms.append((cb * lmat).astype(BF16))
            pl_ = slice(g * gw + e * hp_dim, g * gw + (e + 2) * hp_dim)
            yd = jnp.dot(jnp.concatenate(ms, axis=0), x_in[:, pl_], preferred_element_type=F32)
            yd = jnp.where(first, yd[:q], yd[q:])
            y = yd + y_off[:, e * hp_dim:(e + 2) * hp_dim] * exp_a[:, pl_]
            if epilogue:
                y = y + yp_ref[0, :, pl_] + xs[:, pl_] * dsk_ref[:, pl_]
                y = y * _silu(z_ref[0, :, pl_])
            ys.append(y)

    y_all = jnp.concatenate(ys, axis=1)
    y_ref[0] = _rms(y_all, ng_ref[...]).astype(y_ref.dtype) if epilogue else y_all

    @pl.when(c == pl.num_programs(1) - 1)
    def _():
        sf_ref[0] = s_scr[...]


def ssd_scan_dir(xbc, dt, dt_bias, a_log, s0, reverse, epi=None):
    bsz, n, _ = xbc.shape
    n_heads = dt.shape[-1]
    n_groups = SSM_GROUPS
    hpg = n_heads // n_groups
    d_inner = n_heads * SSM_HEAD_DIM
    gn = n_groups * SSM_D_STATE
    q = SSM_CHUNK
    nc = n // q
    assert d_inner % (2 * gn) == 0 and hpg % 2 == 0 and n % q == 0
    cix = (lambda c: nc - 1 - c) if reverse else (lambda c: c)
    dt_t = dt.swapaxes(1, 2)
    in_specs = [pl.BlockSpec((1, q, d_inner), lambda b, c: (b, cix(c), 0)),
                pl.BlockSpec((1, q, 2 * gn), lambda b, c: (b, cix(c), d_inner // (2 * gn))),
                pl.BlockSpec((1, q, n_heads), lambda b, c: (b, cix(c), 0)),
                pl.BlockSpec((1, n_heads, q), lambda b, c: (b, 0, cix(c))),
                pl.BlockSpec((1, n_heads), lambda b, c: (0, 0)),
                pl.BlockSpec((n_heads, 1), lambda b, c: (0, 0)),
                pl.BlockSpec((1, n_heads), lambda b, c: (0, 0)),
                pl.BlockSpec((n_heads, 1), lambda b, c: (0, 0)),
                pl.BlockSpec((1,) + s0.shape[1:], lambda b, c: (b, 0, 0, 0))]
    args = [xbc, xbc, dt, dt_t, dt_bias.reshape(1, n_heads), dt_bias.reshape(n_heads, 1),
            a_log.reshape(1, n_heads), a_log.reshape(n_heads, 1), s0]
    y_dtype = F32
    if epi is not None:
        y_prev, u, d_skip, norm_g = epi
        in_specs += [pl.BlockSpec((1, q, d_inner), lambda b, c: (b, cix(c), 0)),
                     pl.BlockSpec((1, q, d_inner), lambda b, c: (b, cix(c), 0)),
                     pl.BlockSpec((1, d_inner), lambda b, c: (0, 0)),
                     pl.BlockSpec((1, d_inner), lambda b, c: (0, 0))]
        args += [y_prev, u, jnp.repeat(d_skip, SSM_HEAD_DIM).reshape(1, d_inner), norm_g.reshape(1, d_inner)]
        y_dtype = BF16
    return pl.pallas_call(
        functools.partial(_ssd_kernel, reverse=reverse, epilogue=epi is not None, heads_per_group=hpg),
        grid=(bsz, nc),
        in_specs=in_specs,
        out_specs=[pl.BlockSpec((1, q, d_inner), lambda b, c: (b, cix(c), 0)),
                   pl.BlockSpec((1,) + s0.shape[1:], lambda b, c: (b, 0, 0, 0))],
        out_shape=[jax.ShapeDtypeStruct((bsz, n, d_inner), y_dtype), jax.ShapeDtypeStruct(s0.shape, F32)],
        scratch_shapes=[pltpu.VMEM(s0.shape[1:], F32)],
        compiler_params=_params(2),
        name="ssd_bwd" if reverse else "ssd_fwd",
    )(*args)


def ssd_mixer(u_lat, u_ctx, conv_w, conv_b, dt_bias, a_log, d_skip, norm_g, with_ctx_out):
    bsz = u_lat.shape[0]
    n_heads = a_log.shape[-1]
    d_inner = n_heads * SSM_HEAD_DIM
    gn = SSM_GROUPS * SSM_D_STATE
    dt0 = 2 * d_inner + 2 * gn
    xbc_c = dwconv3(u_ctx, d_inner, conv_w, conv_b, silu=True)
    xbc_l = dwconv3(u_lat, d_inner, conv_w, conv_b, silu=True)
    zero = jnp.zeros((bsz, SSM_GROUPS, SSM_D_STATE, d_inner // SSM_GROUPS), F32)

    def dts(u, d):
        return u[:, :, dt0 + d * n_heads:dt0 + (d + 1) * n_heads]

    epi_args = (d_skip, norm_g)
    y_cf, s_f = ssd_scan_dir(xbc_c, dts(u_ctx, 0), dt_bias[0], a_log[0], zero, False)
    y_c, s_b = ssd_scan_dir(xbc_c, dts(u_ctx, 1), dt_bias[1], a_log[1], zero, True, (y_cf, u_ctx) + epi_args)
    y_lf, _ = ssd_scan_dir(xbc_l, dts(u_lat, 0), dt_bias[0], a_log[0], s_f, False)
    y_l, _ = ssd_scan_dir(xbc_l, dts(u_lat, 1), dt_bias[1], a_log[1], s_b, True, (y_lf, u_lat) + epi_args)
    return y_l, (y_c if with_ctx_out else None)


NEG_BIG = -1e30


def _split_heads(q, first):
    zero = jnp.zeros_like(q)
    return jnp.concatenate([jnp.where(first, q, zero), jnp.where(first, zero, q)], axis=0)


def _na_kernel(q_ref, k_ref, v_ref, kc_ref, vc_ref, bias_ref, o_ref, *, rows, n_pairs):
    r = pl.program_id(1)
    r0 = jnp.clip(r - NA_KH // 2, 0, rows - NA_KH)
    start = pl.multiple_of(r0 * GRID_W, GRID_W)
    band = NA_KH * GRID_W
    scale = NA_HEAD_DIM ** -0.5 * LOG2E
    lane = lax.broadcasted_iota(jnp.int32, (GRID_W, 2 * NA_HEAD_DIM), 1)
    first = lane < NA_HEAD_DIM
    outs = []
    for hp in range(n_pairs):
        cols = slice(hp * 2 * NA_HEAD_DIM, (hp + 1) * 2 * NA_HEAD_DIM)
        q2 = _split_heads((q_ref[0, :, cols].astype(F32) * scale).astype(BF16), first)
        s_lat = lax.dot_general(q2, k_ref[0, pl.ds(start, band), cols], NT_DIMS, preferred_element_type=F32)
        s_lat = s_lat + bias_ref[0, hp]
        s_ctx = lax.dot_general(q2, kc_ref[0, :, cols], NT_DIMS, preferred_element_type=F32)
        m = jnp.maximum(jnp.max(s_lat, axis=-1, keepdims=True), jnp.max(s_ctx, axis=-1, keepdims=True))
        p_lat = jnp.exp2(s_lat - m)
        p_ctx = jnp.exp2(s_ctx - m)
        l = jnp.sum(p_lat, axis=-1, keepdims=True) + jnp.sum(p_ctx, axis=-1, keepdims=True)
        o = (jnp.dot(p_lat.astype(BF16), v_ref[0, pl.ds(start, band), cols], preferred_element_type=F32)
             + jnp.dot(p_ctx.astype(BF16), vc_ref[0, :, cols], preferred_element_type=F32)) / l
        outs.append(jnp.where(first, o[:GRID_W], o[GRID_W:]).astype(o_ref.dtype))
    o_ref[0] = jnp.concatenate(outs, axis=1)


def _na_bias(rpb):
    n_heads = rpb.shape[0]
    col = jnp.arange(GRID_W)
    c0 = jnp.clip(col - NA_KW // 2, 0, GRID_W - NA_KW)
    col_ok = (col[None, :] >= c0[:, None]) & (col[None, :] < c0[:, None] + NA_KW)
    dx_idx = jnp.clip(col[None, :] - col[:, None] + NA_KW - 1, 0, 2 * NA_KW - 2)
    rpb_cols = rpb.astype(F32)[:, :, dx_idx]
    dy_idx = jnp.arange(NA_KH)[None, :] - jnp.arange(NA_KH)[:, None] + NA_KH - 1
    bias = rpb_cols[:, dy_idx]
    bias = jnp.where(col_ok[None, None, None], bias * LOG2E, NEG_BIG)
    return bias.transpose(1, 0, 3, 2, 4).reshape(NA_KH, n_heads // 2, 2 * GRID_W, NA_KH * GRID_W)


def na_attention(u_lat, u_ctx, rpb):
    bsz, n_lat, d3 = u_lat.shape
    d = d3 // 3
    n_ctx = u_ctx.shape[1]
    rows = n_lat // GRID_W
    assert rows >= NA_KH and d % (2 * NA_HEAD_DIM) == 0
    n_pairs = d // (2 * NA_HEAD_DIM)
    bias = _na_bias(rpb)

    def variant(r):
        return r - jnp.clip(r - NA_KH // 2, 0, rows - NA_KH)

    return pl.pallas_call(
        functools.partial(_na_kernel, rows=rows, n_pairs=n_pairs),
        grid=(bsz, rows),
        in_specs=[pl.BlockSpec((1, GRID_W, d), lambda b, r: (b, r, 0)),
                  pl.BlockSpec((1, n_lat, d), lambda b, r: (b, 0, 1)),
                  pl.BlockSpec((1, n_lat, d), lambda b, r: (b, 0, 2)),
                  pl.BlockSpec((1, n_ctx, d), lambda b, r: (b, 0, 1)),
                  pl.BlockSpec((1, n_ctx, d), lambda b, r: (b, 0, 2)),
                  pl.BlockSpec((1, n_pairs, 2 * GRID_W, NA_KH * GRID_W), lambda b, r: (variant(r), 0, 0, 0))],
        out_specs=pl.BlockSpec((1, GRID_W, d), lambda b, r: (b, r, 0)),
        out_shape=jax.ShapeDtypeStruct((bsz, n_lat, d), BF16),
        compiler_params=_params(2),
        name="na_attention",
    )(u_lat, u_lat, u_lat, u_ctx, u_ctx, bias)


def _mha_ctx_kernel(q_ref, k_ref, v_ref, o_ref, *, n_pairs):
    n = q_ref.shape[1]
    scale = NA_HEAD_DIM ** -0.5
    lane = lax.broadcasted_iota(jnp.int32, (n, 2 * NA_HEAD_DIM), 1)
    first = lane < NA_HEAD_DIM
    for hp in range(n_pairs):
        cols = slice(hp * 2 * NA_HEAD_DIM, (hp + 1) * 2 * NA_HEAD_DIM)
        q2 = _split_heads(q_ref[0, :, cols], first)
        s = lax.dot_general(q2, k_ref[0, :, cols], NT_DIMS, preferred_element_type=F32) * scale
        p = jnp.exp(s - jnp.max(s, axis=-1, keepdims=True))
        l = jnp.sum(p, axis=-1, keepdims=True)
        o = jnp.dot(p.astype(BF16), v_ref[0, :, cols], preferred_element_type=F32) / l
        o_ref[0, :, cols] = jnp.where(first, o[:n], o[n:]).astype(o_ref.dtype)


def mha_ctx(u_ctx):
    bsz, n, d3 = u_ctx.shape
    d = d3 // 3
    n_pairs = d // (2 * NA_HEAD_DIM)
    return pl.pallas_call(
        functools.partial(_mha_ctx_kernel, n_pairs=n_pairs),
        grid=(bsz,),
        in_specs=[pl.BlockSpec((1, n, d), lambda b: (b, 0, 0)),
                  pl.BlockSpec((1, n, d), lambda b: (b, 0, 1)),
                  pl.BlockSpec((1, n, d), lambda b: (b, 0, 2))],
        out_specs=pl.BlockSpec((1, n, d), lambda b: (b, 0, 0)),
        out_shape=jax.ShapeDtypeStruct((bsz, n, d), BF16),
        compiler_params=_params(1),
        name="mha_ctx",
    )(u_ctx, u_ctx, u_ctx)


def kernel(x, c, ctx, c_ctx, ada_w, ada_b, norm_g, mlp_w1, mlp_w2, ga_w_in, ga_q_gain, ga_k_gain, ga_w_out, hy_w_in, hy_conv_w, hy_conv_b, hy_f_w1, hy_f_b1, hy_f_w2, hy_f_b2, hy_f_w3, hy_f_freq, hy_skip, hy_w_out, ssm_w_in, ssm_conv_w, ssm_conv_b, ssm_dt_bias, ssm_a_log, ssm_d, ssm_norm_g, ssm_w_out, na_w_in, na_rpb, na_w_out):
    bsz, n_lat, d = x.shape
    n_ctx = ctx.shape[1]
    depth = ada_w.shape[0]
    sc = jnp.concatenate([jax.nn.silu(c), jax.nn.silu(c_ctx)[None, :]], axis=0)
    x_lat = x.reshape(bsz * n_lat, d)
    x_ctx = ctx.reshape(bsz * n_ctx, d)
    for i in range(depth):
        kind, j = i % N_MIXERS, i // N_MIXERS
        with_ctx = i < depth - 1
        mod = dense_f32(sc, ada_w[i], ada_b[i][None, :]).reshape(bsz + 1, 6, d)
        mod_l, mod_c = mod[:bsz], mod[bsz:]
        g = norm_g[i]
        w_in, w_out = ((ga_w_in, ga_w_out), (hy_w_in, hy_w_out), (ssm_w_in, ssm_w_out), (na_w_in, na_w_out))[kind]
        w_in, w_out = w_in[j].astype(BF16), w_out[j].astype(BF16)
        u_dtype = BF16 if kind == 3 else F32
        u_l = norm_proj(x_lat, mod_l, g, w_in, n_lat, u_dtype).reshape(bsz, n_lat, -1)
        u_c = norm_proj(x_ctx, mod_c, g, w_in, bsz * n_ctx, u_dtype).reshape(bsz, n_ctx, -1)
        if kind == 0:
            a_l = gqa_attention(u_l, u_l, u_c, ga_q_gain[j], ga_k_gain[j])
            a_c = gqa_attention(u_c, None, u_c, ga_q_gain[j], ga_k_gain[j]) if with_ctx else None
        elif kind == 1:
            hy = (hy_conv_w[j], hy_conv_b[j], hy_f_w1[j], hy_f_b1[j], hy_f_w2[j], hy_f_b2[j], hy_f_w3[j],
                  hy_f_freq[j], hy_skip[j])
            a_l = hyena_mixer(u_l, *hy)
            a_c = hyena_mixer(u_c, *hy) if with_ctx else None
        elif kind == 2:
            a_l, a_c = ssd_mixer(u_l, u_c, ssm_conv_w[j], ssm_conv_b[j], ssm_dt_bias[j], ssm_a_log[j],
                                 ssm_d[j], ssm_norm_g[j], with_ctx)
        else:
            a_l = na_attention(u_l, u_c, na_rpb[j])
            a_c = mha_ctx(u_c) if with_ctx else None
        w1, w2 = mlp_w1[i].astype(BF16), mlp_w2[i].astype(BF16)
        x_lat = proj_mlp_residual(a_l.reshape(bsz * n_lat, -1), w_out, w1, w2, x_lat, mod_l, g, n_lat)
        if with_ctx:
            x_ctx = proj_mlp_residual(a_c.reshape(bsz * n_ctx, -1), w_out, w1, w2, x_ctx, mod_c, g, bsz * n_ctx)
    return x_lat.reshape(bsz, n_lat, d)
```

```python
import functools
import math

import jax
import jax.numpy as jnp
from jax import lax
from jax.experimental import pallas as pl
from jax.experimental.pallas import tpu as pltpu

GRID_W = 64
N_MIXERS = 4
RMS_EPS = 1e-6

GA_HEAD_DIM = 128
GA_N_KV = 2
ROPE_THETA = 10000.0

HY_ORDER = 2
HY_BANDS = 16
HY_DECAY_SHORT = 0.3
HY_DECAY_LONG = 1.5
HY_TARGET = 1e-2
HY_TILE = 128
HY_CH_BLOCK = 8

SSM_HEAD_DIM = 64
SSM_GROUPS = 4
SSM_D_STATE = 128
SSM_CHUNK = 128

NA_HEAD_DIM = 64
NA_KH = 8
NA_KW = 16

VMEM_LIMIT_BYTES = 56 * 1024 * 1024
BF16 = jnp.bfloat16
F32 = jnp.float32
HIGHEST = lax.Precision.HIGHEST
NT_DIMS = (((1,), (1,)), ((), ()))
LOG2E = 1.4426950408889634


def _params(n_grid_dims):
    return pltpu.CompilerParams(dimension_semantics=("arbitrary",) * n_grid_dims,
                                vmem_limit_bytes=VMEM_LIMIT_BYTES)


def _row_tile(rows_per_mod, cap):
    t = min(rows_per_mod, cap)
    while rows_per_mod % t:
        t //= 2
    return t


def _col_chunks(n, width):
    return [(c, min(width, n - c)) for c in range(0, n, width)]


def _rms(x, g):
    return x * lax.rsqrt(jnp.mean(x * x, axis=-1, keepdims=True) + RMS_EPS) * g


def _silu(x):
    return x / (1.0 + jnp.exp(-x))


def _dense_kernel(a_ref, w_ref, b_ref, o_ref):
    o_ref[...] = jnp.dot(a_ref[...], w_ref[...], preferred_element_type=F32, precision=HIGHEST) + b_ref[...]


def dense_f32(a, w, b):
    m, k = a.shape
    n = w.shape[1]
    tn = 1024 if n % 1024 == 0 else n
    return pl.pallas_call(
        _dense_kernel,
        grid=(n // tn,),
        in_specs=[pl.BlockSpec((m, k), lambda j: (0, 0)),
                  pl.BlockSpec((k, tn), lambda j: (0, j)),
                  pl.BlockSpec((1, tn), lambda j: (0, j))],
        out_specs=pl.BlockSpec((m, tn), lambda j: (0, j)),
        out_shape=jax.ShapeDtypeStruct((m, n), F32),
        compiler_params=_params(1),
        name="ada_dense",
    )(a, w, b)


def _norm_proj_kernel(x_ref, mod_ref, g_ref, w_ref, o_ref, *, tn):
    x = x_ref[...]
    h = _rms(x, g_ref[0:1, :]) * (1.0 + mod_ref[0, 1:2, :]) + mod_ref[0, 0:1, :]
    h = h.astype(BF16)
    for c, s in _col_chunks(w_ref.shape[1], tn):
        o_ref[:, c:c + s] = jnp.dot(h, w_ref[:, c:c + s], preferred_element_type=F32).astype(o_ref.dtype)


def norm_proj(x, mod, g, w, rows_per_mod, out_dtype=F32):
    m, d = x.shape
    n = w.shape[1]
    tm = _row_tile(rows_per_mod, 512)
    bpm = rows_per_mod // tm
    return pl.pallas_call(
        functools.partial(_norm_proj_kernel, tn=512),
        grid=(m // tm,),
        in_specs=[pl.BlockSpec((tm, d), lambda i: (i, 0)),
                  pl.BlockSpec((1, 6, d), lambda i: (i // bpm, 0, 0)),
                  pl.BlockSpec((4, d), lambda i: (0, 0)),
                  pl.BlockSpec((d, n), lambda i: (0, 0))],
        out_specs=pl.BlockSpec((tm, n), lambda i: (i, 0)),
        out_shape=jax.ShapeDtypeStruct((m, n), out_dtype),
        compiler_params=_params(1),
        name="norm_proj",
    )(x, mod, g, w)


def _out_proj_kernel(a_ref, w_ref, x_ref, mod_ref, g_ref, xo_ref, h_ref):
    y = jnp.dot(a_ref[...].astype(BF16), w_ref[...], preferred_element_type=F32)
    x = x_ref[...] + mod_ref[0, 2:3, :] * _rms(y, g_ref[1:2, :])
    xo_ref[...] = x
    h = _rms(x, g_ref[2:3, :]) * (1.0 + mod_ref[0, 4:5, :]) + mod_ref[0, 3:4, :]
    h_ref[...] = h.astype(BF16)


def out_proj_residual(a, w, x, mod, g, rows_per_mod):
    m, k = a.shape
    d = w.shape[1]
    tm = _row_tile(rows_per_mod, 512)
    bpm = rows_per_mod // tm
    return pl.pallas_call(
        _out_proj_kernel,
        grid=(m // tm,),
        in_specs=[pl.BlockSpec((tm, k), lambda i: (i, 0)),
                  pl.BlockSpec((k, d), lambda i: (0, 0)),
                  pl.BlockSpec((tm, d), lambda i: (i, 0)),
                  pl.BlockSpec((1, 6, d), lambda i: (i // bpm, 0, 0)),
                  pl.BlockSpec((4, d), lambda i: (0, 0))],
        out_specs=[pl.BlockSpec((tm, d), lambda i: (i, 0)),
                   pl.BlockSpec((tm, d), lambda i: (i, 0))],
        out_shape=[jax.ShapeDtypeStruct((m, d), F32), jax.ShapeDtypeStruct((m, d), BF16)],
        compiler_params=_params(1),
        name="out_proj",
    )(a, w, x, mod, g)


def _mlp_kernel(h_ref, w1_ref, w2_ref, x_ref, mod_ref, g_ref, o_ref, *, fc):
    h = h_ref[...]
    acc = jnp.zeros(o_ref.shape, F32)
    for c, s in _col_chunks(w1_ref.shape[1], fc):
        u = jnp.dot(h, w1_ref[:, c:c + s], preferred_element_type=F32)
        u = jnp.square(jnp.maximum(u, 0.0)).astype(BF16)
        acc = acc + jnp.dot(u, w2_ref[c:c + s, :], preferred_element_type=F32)
    o_ref[...] = x_ref[...] + mod_ref[0, 5:6, :] * _rms(acc, g_ref[3:4, :])


def mlp_residual(h, w1, w2, x, mod, g, rows_per_mod):
    m, d = x.shape
    f = w1.shape[1]
    tm = _row_tile(rows_per_mod, 512)
    bpm = rows_per_mod // tm
    return pl.pallas_call(
        functools.partial(_mlp_kernel, fc=1024),
        grid=(m // tm,),
        in_specs=[pl.BlockSpec((tm, d), lambda i: (i, 0)),
                  pl.BlockSpec((d, f), lambda i: (0, 0)),
                  pl.BlockSpec((f, d), lambda i: (0, 0)),
                  pl.BlockSpec((tm, d), lambda i: (i, 0)),
                  pl.BlockSpec((1, 6, d), lambda i: (i // bpm, 0, 0)),
                  pl.BlockSpec((4, d), lambda i: (0, 0))],
        out_specs=pl.BlockSpec((tm, d), lambda i: (i, 0)),
        out_shape=jax.ShapeDtypeStruct((m, d), F32),
        compiler_params=_params(1),
        name="mlp",
    )(h, w1, w2, x, mod, g)


def _proj_mlp_kernel(a_ref, wo_ref, w1_ref, w2_ref, x_ref, mod_ref, g_ref, o_ref, *, fc):
    y = jnp.dot(a_ref[...].astype(BF16), wo_ref[...], preferred_element_type=F32)
    x = x_ref[...] + mod_ref[0, 2:3, :] * _rms(y, g_ref[1:2, :])
    h = (_rms(x, g_ref[2:3, :]) * (1.0 + mod_ref[0, 4:5, :]) + mod_ref[0, 3:4, :]).astype(BF16)
    acc = jnp.zeros(o_ref.shape, F32)
    for c, s in _col_chunks(w1_ref.shape[1], fc):
        u = jnp.dot(h, w1_ref[:, c:c + s], preferred_element_type=F32)
        u = jnp.square(jnp.maximum(u, 0.0)).astype(BF16)
        acc = acc + jnp.dot(u, w2_ref[c:c + s, :], preferred_element_type=F32)
    o_ref[...] = x + mod_ref[0, 5:6, :] * _rms(acc, g_ref[3:4, :])


def proj_mlp_residual(a, wo, w1, w2, x, mod, g, rows_per_mod):
    m, k = a.shape
    d = wo.shape[1]
    f = w1.shape[1]
    tm = _row_tile(rows_per_mod, 512)
    bpm = rows_per_mod // tm
    resident = lambda shape: pl.BlockSpec(shape, lambda i: (0, 0), pipeline_mode=pl.Buffered(1))
    return pl.pallas_call(
        functools.partial(_proj_mlp_kernel, fc=1024),
        grid=(m // tm,),
        in_specs=[pl.BlockSpec((tm, k), lambda i: (i, 0)),
                  resident((k, d)), resident((d, f)), resident((f, d)),
                  pl.BlockSpec((tm, d), lambda i: (i, 0)),
                  pl.BlockSpec((1, 6, d), lambda i: (i // bpm, 0, 0)),
                  pl.BlockSpec((4, d), lambda i: (0, 0))],
        out_specs=pl.BlockSpec((tm, d), lambda i: (i, 0)),
        out_shape=jax.ShapeDtypeStruct((m, d), F32),
        compiler_params=_params(1),
        name="proj_mlp",
    )(a, wo, w1, w2, x, mod, g)


def _rope(x, cos, sin_lo, sin_hi):
    return x * cos + pltpu.roll(x, 96, 1) * sin_lo + pltpu.roll(x, 32, 1) * sin_hi


def _gqa_kernel(*refs, n_lat, group):
    hd = GA_HEAD_DIM
    if n_lat:
        (q_ref, kl_ref, vl_ref, kc_ref, vc_ref, qg_ref, kg_ref, tq_ref, tk_ref, o_ref, k_scr, v_scr) = refs
    else:
        (q_ref, kc_ref, vc_ref, qg_ref, kg_ref, o_ref, k_scr, v_scr) = refs

    @pl.when(pl.program_id(2) == 0)
    def _():
        if n_lat:
            k = _rope(_rms(kl_ref[0], kg_ref[...]), tk_ref[0], tk_ref[1], tk_ref[2])
            k_scr[0:n_lat, :] = k.astype(BF16)
            v_scr[0:n_lat, :] = vl_ref[0].astype(BF16)
        k_scr[n_lat:, :] = _rms(kc_ref[0], kg_ref[...]).astype(BF16)
        v_scr[n_lat:, :] = vc_ref[0].astype(BF16)

    outs = []
    for h in range(group):
        q = _rms(q_ref[0, :, h * hd:(h + 1) * hd], qg_ref[...])
        if n_lat:
            q = _rope(q, tq_ref[0], tq_ref[1], tq_ref[2])
        q = (q * (hd ** -0.5 * LOG2E)).astype(BF16)
        s = lax.dot_general(q, k_scr[...], NT_DIMS, preferred_element_type=F32)
        p = jnp.exp2(s - jnp.max(s, axis=-1, keepdims=True))
        l = jnp.sum(p, axis=-1, keepdims=True)
        o = jnp.dot(p.astype(BF16), v_scr[...], preferred_element_type=F32) / l
        outs.append(o.astype(o_ref.dtype))
    o_ref[0] = jnp.concatenate(outs, axis=1)


def _rope_tables(n_tokens, head_dim):
    t = jnp.arange(n_tokens)
    row = (t // GRID_W).astype(F32)
    col = (t % GRID_W).astype(F32)
    nf = head_dim // 4
    inv = ROPE_THETA ** (-jnp.arange(nf, dtype=F32) / nf)
    ar = row[:, None] * inv[None, :]
    ac = col[:, None] * inv[None, :]
    zero = jnp.zeros_like(ar)
    cos = jnp.concatenate([jnp.cos(ar), jnp.cos(ar), jnp.cos(ac), jnp.cos(ac)], axis=-1)
    sin_lo = jnp.concatenate([-jnp.sin(ar), zero, -jnp.sin(ac), zero], axis=-1)
    sin_hi = jnp.concatenate([zero, jnp.sin(ar), zero, jnp.sin(ac)], axis=-1)
    return jnp.stack([cos, sin_lo, sin_hi])


def gqa_attention(u_q, u_lat, u_ctx, q_gain, k_gain):
    hd = GA_HEAD_DIM
    bsz, nq, width = u_q.shape
    n_heads = width // hd - 2 * GA_N_KV
    group = n_heads // GA_N_KV
    n_lat = 0 if u_lat is None else u_lat.shape[1]
    n_ctx = u_ctx.shape[1]
    tq = _row_tile(nq, 512)
    k_col, v_col = n_heads, n_heads + GA_N_KV
    qg, kg = q_gain.reshape(1, hd), k_gain.reshape(1, hd)
    q_spec = pl.BlockSpec((1, tq, group * hd), lambda b, kv, i: (b, i, kv))
    ctx_specs = [pl.BlockSpec((1, n_ctx, hd), lambda b, kv, i: (b, 0, k_col + kv)),
                 pl.BlockSpec((1, n_ctx, hd), lambda b, kv, i: (b, 0, v_col + kv))]
    gain_specs = [pl.BlockSpec((1, hd), lambda b, kv, i: (0, 0))] * 2
    if n_lat:
        tables = _rope_tables(n_lat, hd)
        in_specs = ([q_spec,
                     pl.BlockSpec((1, n_lat, hd), lambda b, kv, i: (b, 0, k_col + kv)),
                     pl.BlockSpec((1, n_lat, hd), lambda b, kv, i: (b, 0, v_col + kv))]
                    + ctx_specs + gain_specs
                    + [pl.BlockSpec((3, tq, hd), lambda b, kv, i: (0, i, 0)),
                       pl.BlockSpec((3, n_lat, hd), lambda b, kv, i: (0, 0, 0))])
        args = (u_q, u_lat, u_lat, u_ctx, u_ctx, qg, kg, tables, tables)
    else:
        in_specs = [q_spec] + ctx_specs + gain_specs
        args = (u_q, u_ctx, u_ctx, qg, kg)
    return pl.pallas_call(
        functools.partial(_gqa_kernel, n_lat=n_lat, group=group),
        grid=(bsz, GA_N_KV, nq // tq),
        in_specs=in_specs,
        out_specs=pl.BlockSpec((1, tq, group * hd), lambda b, kv, i: (b, i, kv)),
        out_shape=jax.ShapeDtypeStruct((bsz, nq, n_heads * hd), BF16),
        scratch_shapes=[pltpu.VMEM((n_lat + n_ctx, hd), BF16), pltpu.VMEM((n_lat + n_ctx, hd), BF16)],
        compiler_params=_params(3),
        name="gqa_lat" if n_lat else "gqa_ctx",
    )(*args)


def _dwconv_kernel(u_ref, w_ref, b_ref, o_ref, *, silu, transpose_out):
    x = u_ref[0]
    n = x.shape[0]
    row = lax.broadcasted_iota(jnp.int32, x.shape, 0)
    prev = jnp.where(row == 0, 0.0, pltpu.roll(x, 1, 0))
    nxt = jnp.where(row == n - 1, 0.0, pltpu.roll(x, n - 1, 0))
    y = prev * w_ref[0:1, :] + x * w_ref[1:2, :] + nxt * w_ref[2:3, :] + b_ref[...]
    if silu:
        y = _silu(y)
    if transpose_out:
        o_ref[...] = y.T.astype(o_ref.dtype)
    else:
        o_ref[0] = y.astype(o_ref.dtype)


def dwconv3(u, col0, w, b, silu, transpose_out=False):
    bsz, n, _ = u.shape
    ch = w.shape[1]
    tc = 512
    assert ch % tc == 0 and col0 % tc == 0 and w.shape[0] == 3
    if transpose_out:
        out_spec = pl.BlockSpec((tc, n), lambda b_, j: (j, b_))
        out_shape = jax.ShapeDtypeStruct((ch, bsz * n), F32)
    else:
        out_spec = pl.BlockSpec((1, n, tc), lambda b_, j: (b_, 0, j))
        out_shape = jax.ShapeDtypeStruct((bsz, n, ch), F32)
    out = pl.pallas_call(
        functools.partial(_dwconv_kernel, silu=silu, transpose_out=transpose_out),
        grid=(bsz, ch // tc),
        in_specs=[pl.BlockSpec((1, n, tc), lambda b_, j: (b_, 0, col0 // tc + j)),
                  pl.BlockSpec((3, tc), lambda b_, j: (0, j)),
                  pl.BlockSpec((1, tc), lambda b_, j: (0, j))],
        out_specs=out_spec,
        out_shape=out_shape,
        compiler_params=_params(2),
        name="dwconv3",
    )(u, w, b.reshape(1, ch))
    return out.reshape(ch, bsz, n) if transpose_out else out


def _hyena_filter_kernel(feats_ref, w1_ref, b1_ref, w2_ref, b2_ref, fr_ref, w3_ref, dl_ref, o_ref, h_scr):
    @pl.when(pl.program_id(0) == 0)
    def _():
        h = jnp.dot(w1_ref[...], feats_ref[...], precision=HIGHEST, preferred_element_type=F32) + b1_ref[...]
        h = jnp.sin(fr_ref[:, 0:1] * h)
        h = jnp.dot(w2_ref[...], h, precision=HIGHEST, preferred_element_type=F32) + b2_ref[...]
        h_scr[...] = jnp.sin(fr_ref[:, 1:2] * h)

    h = jnp.dot(w3_ref[...], h_scr[...], precision=HIGHEST, preferred_element_type=F32)
    o_ref[...] = h * jnp.exp(-dl_ref[...] * feats_ref[0:1, :])


def hyena_filters(n, d, w1, b1, w2, b2, w3, freq):
    t = jnp.linspace(0.0, 1.0, n, dtype=F32)[None, :]
    w = 2.0 * math.pi * jnp.arange(n, dtype=F32)[None, :] / n
    bands = jnp.linspace(1e-4, HY_BANDS - 1, HY_BANDS, dtype=F32)[:, None]
    feats = jnp.concatenate([t, jnp.cos(bands * w), -jnp.sin(bands * w)], axis=0)
    deltas = jnp.abs(jnp.linspace(math.log(HY_TARGET) / HY_DECAY_SHORT,
                                  math.log(HY_TARGET) / HY_DECAY_LONG, d, dtype=F32))
    n_f = 2 * HY_ORDER * d
    dl = jnp.tile(deltas, 2 * HY_ORDER).reshape(n_f, 1)
    hid = w1.shape[1]
    emb = -(-feats.shape[0] // 8) * 8
    feats = jnp.pad(feats, ((0, emb - feats.shape[0]), (0, 0)))
    w1 = jnp.pad(w1, ((0, emb - w1.shape[0]), (0, 0)))
    tr = 512
    assert n_f % tr == 0
    const = lambda shape: pl.BlockSpec(shape, lambda j: (0, 0))
    return pl.pallas_call(
        _hyena_filter_kernel,
        grid=(n_f // tr,),
        in_specs=[const((emb, n)), const((hid, emb)), const((hid, 1)), const((hid, hid)), const((hid, 1)),
                  const((hid, 2)),
                  pl.BlockSpec((tr, hid), lambda j: (j, 0)),
                  pl.BlockSpec((tr, 1), lambda j: (j, 0))],
        out_specs=pl.BlockSpec((tr, n), lambda j: (j, 0)),
        out_shape=jax.ShapeDtypeStruct((n_f, n), F32),
        scratch_shapes=[pltpu.VMEM((hid, n), F32)],
        compiler_params=_params(1),
        name="hyena_filter",
    )(feats, w1.T, b1.reshape(hid, 1), w2.T, b2.reshape(hid, 1), freq.T, w3.T, dl)


def _toeplitz_conv(z, gline):
    bsz, n = z.shape
    half, t = HY_TILE, 2 * HY_TILE
    m = n // t
    g = pltpu.roll(jnp.broadcast_to(gline, (half, 2 * n)), 0, 1, stride=1, stride_axis=0).astype(BF16)
    zb = z.astype(BF16)
    zr = jnp.concatenate([zb[:, j * t:(j + 1) * t] for j in range(m)], axis=0)
    acc = [None] * m
    for dd in range(-(m - 1), m):
        lo = n + t * dd
        tile = jnp.concatenate([g[:, lo:lo + t], g[:, lo - half:lo - half + t]], axis=0)
        j0, cnt = max(0, -dd), m - abs(dd)
        out = jnp.dot(zr[j0 * bsz:(j0 + cnt) * bsz], tile, preferred_element_type=F32)
        for k in range(cnt):
            i = j0 + dd + k
            piece = out[k * bsz:(k + 1) * bsz]
            acc[i] = piece if acc[i] is None else acc[i] + piece
    return jnp.concatenate(acc, axis=1)


def _hyena_conv_kernel(v_ref, x1_ref, x2_ref, g1_ref, g2_ref, sk_ref, o_ref):
    def body(c, carry):
        v = v_ref[c]
        z = x1_ref[c] * (_toeplitz_conv(v, g1_ref[pl.ds(c, 1), :]) + v * sk_ref[pl.ds(c, 1), 0:1])
        z = x2_ref[c] * (_toeplitz_conv(z, g2_ref[pl.ds(c, 1), :]) + z * sk_ref[pl.ds(c, 1), 1:2])
        o_ref[c] = z.astype(o_ref.dtype)
        return carry

    lax.fori_loop(0, v_ref.shape[0], body, 0, unroll=4)


def hyena_mixer(u, conv_w, conv_b, f_w1, f_b1, f_w2, f_b2, f_w3, f_freq, skip):
    bsz, n, d3 = u.shape
    d = d3 // (HY_ORDER + 1)
    dc = HY_CH_BLOCK
    assert d % dc == 0 and n % (2 * HY_TILE) == 0
    uc = dwconv3(u, 0, conv_w, conv_b, silu=False, transpose_out=True)
    filt = hyena_filters(n, d, f_w1, f_b1, f_w2, f_b2, f_w3, f_freq).reshape(2 * HY_ORDER, d, n)

    def gline(fwd, bwd):
        return jnp.concatenate([jnp.zeros((d, 1), F32), jnp.flip(bwd[:, 1:], axis=1), fwd], axis=1)

    g1, g2 = gline(filt[0], filt[1]), gline(filt[2], filt[3])
    nb = d // dc
    z = pl.pallas_call(
        _hyena_conv_kernel,
        grid=(nb,),
        in_specs=[pl.BlockSpec((dc, bsz, n), lambda j: (j, 0, 0)),
                  pl.BlockSpec((dc, bsz, n), lambda j: (nb + j, 0, 0)),
                  pl.BlockSpec((dc, bsz, n), lambda j: (2 * nb + j, 0, 0)),
                  pl.BlockSpec((dc, 2 * n), lambda j: (j, 0)),
                  pl.BlockSpec((dc, 2 * n), lambda j: (j, 0)),
                  pl.BlockSpec((dc, HY_ORDER), lambda j: (j, 0))],
        out_specs=pl.BlockSpec((dc, bsz, n), lambda j: (j, 0, 0)),
        out_shape=jax.ShapeDtypeStruct((d, bsz, n), BF16),
        compiler_params=_params(1),
        name="hyena_conv",
    )(uc, uc, uc, g1, g2, skip.T)
    return z.transpose(1, 2, 0)


def _softplus(x):
    return jnp.maximum(x, 0.0) + jnp.log(1.0 + jnp.exp(-jnp.abs(x)))


def _ssd_kernel(*refs, reverse, epilogue, heads_per_group):
    hp_dim, ns, q = SSM_HEAD_DIM, SSM_D_STATE, SSM_CHUNK
    if epilogue:
        (xs_ref, bc_ref, dt_ref, dtt_ref, dtb_ref, dtbt_ref, al_ref, alt_ref, s0_ref,
         yp_ref, z_ref, dsk_ref, ng_ref, y_ref, sf_ref, s_scr) = refs
    else:
        (xs_ref, bc_ref, dt_ref, dtt_ref, dtb_ref, dtbt_ref, al_ref, alt_ref, s0_ref,
         y_ref, sf_ref, s_scr) = refs
    c = pl.program_id(1)
    n_groups = s_scr.shape[0]
    n_heads = n_groups * heads_per_group
    gw = heads_per_group * hp_dim

    @pl.when(c == 0)
    def _():
        s_scr[...] = s0_ref[0]

    li = lax.broadcasted_iota(jnp.int32, (q, q), 0)
    si = lax.broadcasted_iota(jnp.int32, (q, q), 1)
    keep = (si >= li) if reverse else (si <= li)
    tri = keep.astype(F32)
    tri_t = ((li >= si) if reverse else (li <= si)).astype(F32)
    edge = 0 if reverse else q - 1

    dt = _softplus(dt_ref[0] + dtb_ref[...])
    da = dt * -jnp.exp(al_ref[...])
    a_cs = jnp.dot(tri, da, precision=HIGHEST, preferred_element_type=F32)
    da_t = _softplus(dtt_ref[0] + dtbt_ref[...]) * -jnp.exp(alt_ref[...])
    a_cs_t = jnp.dot(da_t, tri_t, precision=HIGHEST, preferred_element_type=F32)

    eh = lax.broadcasted_iota(jnp.int32, (n_heads, n_heads * hp_dim), 0)
    el = lax.broadcasted_iota(jnp.int32, (n_heads, n_heads * hp_dim), 1)
    expand_p = (el // hp_dim == eh).astype(BF16)
    eh2 = lax.broadcasted_iota(jnp.int32, (n_heads, n_heads * q), 0)
    el2 = lax.broadcasted_iota(jnp.int32, (n_heads, n_heads * q), 1)
    expand_q = (el2 // q == eh2).astype(BF16)

    def spread(v, e):
        return jnp.dot(v.astype(BF16), e, preferred_element_type=F32)

    a_hi = a_cs.astype(BF16)
    a_r1 = a_cs - a_hi.astype(F32)
    a_mid = a_r1.astype(BF16)
    a_lo = a_r1 - a_mid.astype(F32)
    acs_tile = spread(a_hi, expand_q) + spread(a_mid, expand_q) + spread(a_lo, expand_q)
    tot = a_cs[edge:edge + 1, :]
    dt_full = spread(dt, expand_p)
    st_full = spread(jnp.exp(tot - a_cs), expand_p)
    exp_a = spread(jnp.exp(a_cs), expand_p)
    tot_full = jnp.dot(jnp.broadcast_to(tot, (8, n_heads)), expand_p.astype(F32), precision=HIGHEST,
                       preferred_element_type=F32)[0:1, :]

    xs = xs_ref[0]
    x_dt = xs * dt_full
    x_in = x_dt.astype(BF16)
    x_st = (x_dt * st_full).astype(BF16)
    lane = lax.broadcasted_iota(jnp.int32, (q, 2 * hp_dim), 1)
    first = lane < hp_dim
    gn = n_groups * ns
    ys = []
    for g in range(n_groups):
        b_g = bc_ref[0, :, g * ns:(g + 1) * ns]
        c_g = bc_ref[0, :, gn + g * ns:gn + (g + 1) * ns].astype(BF16)
        cb = lax.dot_general(c_g, b_g.astype(BF16), NT_DIMS, preferred_element_type=F32)
        s_prev = s_scr[g]
        y_off = jnp.dot(c_g, s_prev.astype(BF16), preferred_element_type=F32)
        gl = slice(g * gw, (g + 1) * gw)
        s_new = jnp.dot(b_g.T.astype(BF16), x_st[:, gl], preferred_element_type=F32)
        s_scr[g] = jnp.exp(tot_full[:, gl]) * s_prev + s_new
        for e in range(0, heads_per_group, 2):
            ms = []
            for h in (g * heads_per_group + e, g * heads_per_group + e + 1):
                diff = acs_tile[:, h * q:(h + 1) * q] - a_cs_t[h:h + 1, :]
                lmat = jnp.where(keep, jnp.exp(jnp.minimum(diff, 0.0)), 0.0)
                ms.append((cb * lmat).astype(BF16))
            pl_ = slice(g * gw + e * hp_dim, g * gw + (e + 2) * hp_dim)
            yd = jnp.dot(jnp.concatenate(ms, axis=0), x_in[:, pl_], preferred_element_type=F32)
            yd = jnp.where(first, yd[:q], yd[q:])
            y = yd + y_off[:, e * hp_dim:(e + 2) * hp_dim] * exp_a[:, pl_]
            if epilogue:
                y = y + yp_ref[0, :, pl_] + xs[:, pl_] * dsk_ref[:, pl_]
                y = y * _silu(z_ref[0, :, pl_])
            ys.append(y)

    y_all = jnp.concatenate(ys, axis=1)
    y_ref[0] = _rms(y_all, ng_ref[...]).astype(y_ref.dtype) if epilogue else y_all

    @pl.when(c == pl.num_programs(1) - 1)
    def _():
        sf_ref[0] = s_scr[...]


def ssd_scan_dir(xbc, dt, dt_bias, a_log, s0, reverse, epi=None):
    bsz, n, _ = xbc.shape
    n_heads = dt.shape[-1]
    n_groups = SSM_GROUPS
    hpg = n_heads // n_groups
    d_inner = n_heads * SSM_HEAD_DIM
    gn = n_groups * SSM_D_STATE
    q = SSM_CHUNK
    nc = n // q
    assert d_inner % (2 * gn) == 0 and hpg % 2 == 0 and n % q == 0
    cix = (lambda c: nc - 1 - c) if reverse else (lambda c: c)
    dt_t = dt.swapaxes(1, 2)
    in_specs = [pl.BlockSpec((1, q, d_inner), lambda b, c: (b, cix(c), 0)),
                pl.BlockSpec((1, q, 2 * gn), lambda b, c: (b, cix(c), d_inner // (2 * gn))),
                pl.BlockSpec((1, q, n_heads), lambda b, c: (b, cix(c), 0)),
                pl.BlockSpec((1, n_heads, q), lambda b, c: (b, 0, cix(c))),
                pl.BlockSpec((1, n_heads), lambda b, c: (0, 0)),
                pl.BlockSpec((n_heads, 1), lambda b, c: (0, 0)),
                pl.BlockSpec((1, n_heads), lambda b, c: (0, 0)),
                pl.BlockSpec((n_heads, 1), lambda b, c: (0, 0)),
                pl.BlockSpec((1,) + s0.shape[1:], lambda b, c: (b, 0, 0, 0))]
    args = [xbc, xbc, dt, dt_t, dt_bias.reshape(1, n_heads), dt_bias.reshape(n_heads, 1),
            a_log.reshape(1, n_heads), a_log.reshape(n_heads, 1), s0]
    y_dtype = F32
    if epi is not None:
        y_prev, u, d_skip, norm_g = epi
        in_specs += [pl.BlockSpec((1, q, d_inner), lambda b, c: (b, cix(c), 0)),
                     pl.BlockSpec((1, q, d_inner), lambda b, c: (b, cix(c), 0)),
                     pl.BlockSpec((1, d_inner), lambda b, c: (0, 0)),
                     pl.BlockSpec((1, d_inner), lambda b, c: (0, 0))]
        args += [y_prev, u, jnp.repeat(d_skip, SSM_HEAD_DIM).reshape(1, d_inner), norm_g.reshape(1, d_inner)]
        y_dtype = BF16
    return pl.pallas_call(
        functools.partial(_ssd_kernel, reverse=reverse, epilogue=epi is not None, heads_per_group=hpg),
        grid=(bsz, nc),
        in_specs=in_specs,
        out_specs=[pl.BlockSpec((1, q, d_inner), lambda b, c: (b, cix(c), 0)),
                   pl.BlockSpec((1,) + s0.shape[1:], lambda b, c: (b, 0, 0, 0))],
        out_shape=[jax.ShapeDtypeStruct((bsz, n, d_inner), y_dtype), jax.ShapeDtypeStruct(s0.shape, F32)],
        scratch_shapes=[pltpu.VMEM(s0.shape[1:], F32)],
        compiler_params=_params(2),
        name="ssd_bwd" if reverse else "ssd_fwd",
    )(*args)


def ssd_mixer(u_lat, u_ctx, conv_w, conv_b, dt_bias, a_log, d_skip, norm_g, with_ctx_out):
    bsz = u_lat.shape[0]
    n_heads = a_log.shape[-1]
    d_inner = n_heads * SSM_HEAD_DIM
    gn = SSM_GROUPS * SSM_D_STATE
    dt0 = 2 * d_inner + 2 * gn
    xbc_c = dwconv3(u_ctx, d_inner, conv_w, conv_b, silu=True)
    xbc_l = dwconv3(u_lat, d_inner, conv_w, conv_b, silu=True)
    zero = jnp.zeros((bsz, SSM_GROUPS, SSM_D_STATE, d_inner // SSM_GROUPS), F32)

    def dts(u, d):
        return u[:, :, dt0 + d * n_heads:dt0 + (d + 1) * n_heads]

    epi_args = (d_skip, norm_g)
    y_cf, s_f = ssd_scan_dir(xbc_c, dts(u_ctx, 0), dt_bias[0], a_log[0], zero, False)
    y_c, s_b = ssd_scan_dir(xbc_c, dts(u_ctx, 1), dt_bias[1], a_log[1], zero, True, (y_cf, u_ctx) + epi_args)
    y_lf, _ = ssd_scan_dir(xbc_l, dts(u_lat, 0), dt_bias[0], a_log[0], s_f, False)
    y_l, _ = ssd_scan_dir(xbc_l, dts(u_lat, 1), dt_bias[1], a_log[1], s_b, True, (y_lf, u_lat) + epi_args)
    return y_l, (y_c if with_ctx_out else None)


NEG_BIG = -1e30


def _split_heads(q, first):
    zero = jnp.zeros_like(q)
    return jnp.concatenate([jnp.where(first, q, zero), jnp.where(first, zero, q)], axis=0)


def _na_kernel(q_ref, k_ref, v_ref, kc_ref, vc_ref, bias_ref, o_ref, *, rows, n_pairs):
    r = pl.program_id(1)
    r0 = jnp.clip(r - NA_KH // 2, 0, rows - NA_KH)
    start = pl.multiple_of(r0 * GRID_W, GRID_W)
    band = NA_KH * GRID_W
    scale = NA_HEAD_DIM ** -0.5 * LOG2E
    lane = lax.broadcasted_iota(jnp.int32, (GRID_W, 2 * NA_HEAD_DIM), 1)
    first = lane < NA_HEAD_DIM
    outs = []
    for hp in range(n_pairs):
        cols = slice(hp * 2 * NA_HEAD_DIM, (hp + 1) * 2 * NA_HEAD_DIM)
        q2 = _split_heads((q_ref[0, :, cols].astype(F32) * scale).astype(BF16), first)
        s_lat = lax.dot_general(q2, k_ref[0, pl.ds(start, band), cols], NT_DIMS, preferred_element_type=F32)
        s_lat = s_lat + bias_ref[0, hp]
        s_ctx = lax.dot_general(q2, kc_ref[0, :, cols], NT_DIMS, preferred_element_type=F32)
        m = jnp.maximum(jnp.max(s_lat, axis=-1, keepdims=True), jnp.max(s_ctx, axis=-1, keepdims=True))
        p_lat = jnp.exp2(s_lat - m)
        p_ctx = jnp.exp2(s_ctx - m)
        l = jnp.sum(p_lat, axis=-1, keepdims=True) + jnp.sum(p_ctx, axis=-1, keepdims=True)
        o = (jnp.dot(p_lat.astype(BF16), v_ref[0, pl.ds(start, band), cols], preferred_element_type=F32)
             + jnp.dot(p_ctx.astype(BF16), vc_ref[0, :, cols], preferred_element_type=F32)) / l
        outs.append(jnp.where(first, o[:GRID_W], o[GRID_W:]).astype(o_ref.dtype))
    o_ref[0] = jnp.concatenate(outs, axis=1)


def _na_bias(rpb):
    n_heads = rpb.shape[0]
    col = jnp.arange(GRID_W)
    c0 = jnp.clip(col - NA_KW // 2, 0, GRID_W - NA_KW)
    col_ok = (col[None, :] >= c0[:, None]) & (col[None, :] < c0[:, None] + NA_KW)
    dx_idx = jnp.clip(col[None, :] - col[:, None] + NA_KW - 1, 0, 2 * NA_KW - 2)
    rpb_cols = rpb.astype(F32)[:, :, dx_idx]
    dy_idx = jnp.arange(NA_KH)[None, :] - jnp.arange(NA_KH)[:, None] + NA_KH - 1
    bias = rpb_cols[:, dy_idx]
    bias = jnp.where(col_ok[None, None, None], bias * LOG2E, NEG_BIG)
    return bias.transpose(1, 0, 3, 2, 4).reshape(NA_KH, n_heads // 2, 2 * GRID_W, NA_KH * GRID_W)


def na_attention(u_lat, u_ctx, rpb):
    bsz, n_lat, d3 = u_lat.shape
    d = d3 // 3
    n_ctx = u_ctx.shape[1]
    rows = n_lat // GRID_W
    assert rows >= NA_KH and d % (2 * NA_HEAD_DIM) == 0
    n_pairs = d // (2 * NA_HEAD_DIM)
    bias = _na_bias(rpb)

    def variant(r):
        return r - jnp.clip(r - NA_KH // 2, 0, rows - NA_KH)

    return pl.pallas_call(
        functools.partial(_na_kernel, rows=rows, n_pairs=n_pairs),
        grid=(bsz, rows),
        in_specs=[pl.BlockSpec((1, GRID_W, d), lambda b, r: (b, r, 0)),
                  pl.BlockSpec((1, n_lat, d), lambda b, r: (b, 0, 1)),
                  pl.BlockSpec((1, n_lat, d), lambda b, r: (b, 0, 2)),
                  pl.BlockSpec((1, n_ctx, d), lambda b, r: (b, 0, 1)),
                  pl.BlockSpec((1, n_ctx, d), lambda b, r: (b, 0, 2)),
                  pl.BlockSpec((1, n_pairs, 2 * GRID_W, NA_KH * GRID_W), lambda b, r: (variant(r), 0, 0, 0))],
        out_specs=pl.BlockSpec((1, GRID_W, d), lambda b, r: (b, r, 0)),
        out_shape=jax.ShapeDtypeStruct((bsz, n_lat, d), BF16),
        compiler_params=_params(2),
        name="na_attention",
    )(u_lat, u_lat, u_lat, u_ctx, u_ctx, bias)


def _mha_ctx_kernel(q_ref, k_ref, v_ref, o_ref, *, n_pairs):
    n = q_ref.shape[1]
    scale = NA_HEAD_DIM ** -0.5
    lane = lax.broadcasted_iota(jnp.int32, (n, 2 * NA_HEAD_DIM), 1)
    first = lane < NA_HEAD_DIM
    for hp in range(n_pairs):
        cols = slice(hp * 2 * NA_HEAD_DIM, (hp + 1) * 2 * NA_HEAD_DIM)
        q2 = _split_heads(q_ref[0, :, cols], first)
        s = lax.dot_general(q2, k_ref[0, :, cols], NT_DIMS, preferred_element_type=F32) * scale
        p = jnp.exp(s - jnp.max(s, axis=-1, keepdims=True))
        l = jnp.sum(p, axis=-1, keepdims=True)
        o = jnp.dot(p.astype(BF16), v_ref[0, :, cols], preferred_element_type=F32) / l
        o_ref[0, :, cols] = jnp.where(first, o[:n], o[n:]).astype(o_ref.dtype)


def mha_ctx(u_ctx):
    bsz, n, d3 = u_ctx.shape
    d = d3 // 3
    n_pairs = d // (2 * NA_HEAD_DIM)
    return pl.pallas_call(
        functools.partial(_mha_ctx_kernel, n_pairs=n_pairs),
        grid=(bsz,),
        in_specs=[pl.BlockSpec((1, n, d), lambda b: (b, 0, 0)),
                  pl.BlockSpec((1, n, d), lambda b: (b, 0, 1)),
                  pl.BlockSpec((1, n, d), lambda b: (b, 0, 2))],
        out_specs=pl.BlockSpec((1, n, d), lambda b: (b, 0, 0)),
        out_shape=jax.ShapeDtypeStruct((bsz, n, d), BF16),
        compiler_params=_params(1),
        name="mha_ctx",
    )(u_ctx, u_ctx, u_ctx)


def kernel(x, c, ctx, c_ctx, ada_w, ada_b, norm_g, mlp_w1, mlp_w2, ga_w_in, ga_q_gain, ga_k_gain, ga_w_out, hy_w_in, hy_conv_w, hy_conv_b, hy_f_w1, hy_f_b1, hy_f_w2, hy_f_b2, hy_f_w3, hy_f_freq, hy_skip, hy_w_out, ssm_w_in, ssm_conv_w, ssm_conv_b, ssm_dt_bias, ssm_a_log, ssm_d, ssm_norm_g, ssm_w_out, na_w_in, na_rpb, na_w_out):
    bsz, n_lat, d = x.shape
    n_ctx = ctx.shape[1]
    depth = ada_w.shape[0]
    sc = jnp.concatenate([jax.nn.silu(c), jax.nn.silu(c_ctx)[None, :]], axis=0)
    x_lat = x.reshape(bsz * n_lat, d)
    x_ctx = ctx.reshape(bsz * n_ctx, d)
    for i in range(depth):
        kind, j = i % N_MIXERS, i // N_MIXERS
        with_ctx = i < depth - 1
        mod = dense_f32(sc, ada_w[i], ada_b[i][None, :]).reshape(bsz + 1, 6, d)
        mod_l, mod_c = mod[:bsz], mod[bsz:]
        g = norm_g[i]
        w_in, w_out = ((ga_w_in, ga_w_out), (hy_w_in, hy_w_out), (ssm_w_in, ssm_w_out), (na_w_in, na_w_out))[kind]
        w_in, w_out = w_in[j].astype(BF16), w_out[j].astype(BF16)
        u_dtype = BF16 if kind == 3 else F32
        u_l = norm_proj(x_lat, mod_l, g, w_in, n_lat, u_dtype).reshape(bsz, n_lat, -1)
        u_c = norm_proj(x_ctx, mod_c, g, w_in, bsz * n_ctx, u_dtype).reshape(bsz, n_ctx, -1)
        if kind == 0:
            a_l = gqa_attention(u_l, u_l, u_c, ga_q_gain[j], ga_k_gain[j])
            a_c = gqa_attention(u_c, None, u_c, ga_q_gain[j], ga_k_gain[j]) if with_ctx else None
        elif kind == 1:
            hy = (hy_conv_w[j], hy_conv_b[j], hy_f_w1[j], hy_f_b1[j], hy_f_w2[j], hy_f_b2[j], hy_f_w3[j],
                  hy_f_freq[j], hy_skip[j])
            a_l = hyena_mixer(u_l, *hy)
            a_c = hyena_mixer(u_c, *hy) if with_ctx else None
        elif kind == 2:
            a_l, a_c = ssd_mixer(u_l, u_c, ssm_conv_w[j], ssm_conv_b[j], ssm_dt_bias[j], ssm_a_log[j],
                                 ssm_d[j], ssm_norm_g[j], with_ctx)
        else:
            a_l = na_attention(u_l, u_c, na_rpb[j])
            a_c = mha_ctx(u_c) if with_ctx else None
        w1, w2 = mlp_w1[i].astype(BF16), mlp_w2[i].astype(BF16)
        x_lat = proj_mlp_residual(a_l.reshape(bsz * n_lat, -1), w_out, w1, w2, x_lat, mod_l, g, n_lat)
        if with_ctx:
            x_ctx = proj_mlp_residual(a_c.reshape(bsz * n_ctx, -1), w_out, w1, w2, x_ctx, mod_c, g, bsz * n_ctx)
    return x_lat.reshape(bsz, n_lat, d)
```
